```python
import jax, jax.numpy as jnp
from jax import lax
import numpy as np

D_MODEL = 1024
BATCH = 2
SEQ = 16384
DEPTH = 4

A_HEADS = 8
A_HEAD_DIM = 64
A_WIDTH = A_HEADS * A_HEAD_DIM
DILATED_PATTERNS = ((128, 1), (512, 4), (2048, 16))
A_HALF_STEPS = 64
A_BLOCK = 64
REL_BUCKETS = 32
REL_MAX_DISTANCE = 1024
B_HEADS = 4
B_KEY_DIM = 128
B_VAL_DIM = 256
B_K_WIDTH = B_HEADS * B_KEY_DIM
B_V_WIDTH = B_HEADS * B_VAL_DIM
B_GATE_RANK = 16
B_GATE_NORMALIZER = 16.0
C_HEADS = 12
C_HEAD_DIM = 128
C_WIDTH = C_HEADS * C_HEAD_DIM
MIX_WIDTH = A_WIDTH + B_V_WIDTH
AB_IN_WIDTH = 3 * A_WIDTH + 2 * B_K_WIDTH + B_V_WIDTH + 2 * B_GATE_RANK + MIX_WIDTH
C_IN_WIDTH = 5 * C_WIDTH
CHUNK = 32
N_EVEN = (DEPTH + 1) // 2
N_ODD = DEPTH // 2
NORM_EPS = 1e-5
DEEPNORM_ALPHA = (2 * DEPTH) ** 0.25
DEEPNORM_BETA = (8 * DEPTH) ** -0.25

kernel_name = 'hybrid_dilated_gla_hgrn2_deepnorm_encoder'


def _split(t, sizes):
    out, start = [], 0
    for n in sizes:
        out.append(t[..., start:start + n])
        start += n
    return out


def _t5_bucket(rel):
    half = REL_BUCKETS // 2
    max_exact = half // 2
    n = np.abs(rel)
    large = max_exact + (np.log(np.maximum(n, 1) / max_exact)
                         / np.log(REL_MAX_DISTANCE / max_exact) * (half - max_exact)).astype(np.int32)
    large = np.minimum(large, half - 1)
    return np.where(rel > 0, half, 0) + np.where(n < max_exact, n, large)


def _dilated_band_attention(q, k, v, rel_bias, dilation):
    Bsz, S, H, E = q.shape
    L = S // dilation
    nb = -(-L // A_BLOCK)
    Lp = nb * A_BLOCK

    def to_sub(t):
        return t.reshape(Bsz, L, dilation, H, E).transpose(0, 2, 1, 3, 4)

    qs, ks, vs = to_sub(q), to_sub(k), to_sub(v)
    qs = jnp.pad(qs, ((0, 0), (0, 0), (0, Lp - L), (0, 0), (0, 0))).reshape(Bsz, dilation, nb, A_BLOCK, H, E)

    def band(t):
        t = jnp.pad(t, ((0, 0), (0, 0), (A_BLOCK, Lp - L + A_BLOCK), (0, 0), (0, 0)))
        t = t.reshape(Bsz, dilation, nb + 2, A_BLOCK, H, E)
        return jnp.concatenate([t[:, :, :-2], t[:, :, 1:-1], t[:, :, 2:]], axis=3)

    kb, vb = band(ks), band(vs)
    qi = np.arange(A_BLOCK)[:, None]
    kj = np.arange(3 * A_BLOCK)[None, :]
    off = kj - A_BLOCK - qi
    key_idx = np.arange(nb)[:, None, None] * A_BLOCK + kj[None] - A_BLOCK
    mask = (np.abs(off) <= A_HALF_STEPS)[None] & (key_idx >= 0) & (key_idx < L)
    bias = jnp.transpose(rel_bias.astype(jnp.float32)[_t5_bucket(off * dilation)], (2, 0, 1))
    s = jnp.einsum('brnqhe,brnkhe->brnhqk', qs, kb) * (E ** -0.5) + bias
    s = jnp.where(mask[None, None, :, None], s, -jnp.inf)
    lse = jax.nn.logsumexp(s, axis=-1)
    p = jnp.exp(s - lse[..., None])
    o = jnp.einsum('brnhqk,brnkhe->brnqhe', p, vb).reshape(Bsz, dilation, Lp, H, E)[:, :, :L]
    o = o.transpose(0, 2, 1, 3, 4).reshape(Bsz, S, H, E)
    lse = lse.transpose(0, 1, 2, 4, 3).reshape(Bsz, dilation, Lp, H)[:, :, :L]
    lse = lse.transpose(0, 2, 1, 3).reshape(Bsz, S, H)
    return o, lse


def dilated_attention(q, k, v, rel_bias):
    outs, lses = [], []
    for _, dilation in DILATED_PATTERNS:
        o, lse = _dilated_band_attention(q, k, v, rel_bias, dilation)
        outs.append(o)
        lses.append(lse)
    w = jax.nn.softmax(jnp.stack(lses, 0), axis=0)
    return jnp.einsum('pbsh,pbshe->bshe', w, jnp.stack(outs, 0))


def _chunk_gated_scan(q, k, v, log_g):
    Bsz, S, H, K = q.shape
    V = v.shape[-1]
    N = S // CHUNK
    c = lambda t: t.reshape(Bsz, N, CHUNK, H, t.shape[-1])
    q, k, v, log_g = c(q), c(k), c(v), c(log_g)
    b = jnp.cumsum(log_g, axis=2)
    ref = b[:, :, CHUNK // 2 - 1:CHUNK // 2]
    s = jnp.einsum('bnthk,bnshk->bnhts', q * jnp.exp(b - ref), k * jnp.exp(ref - b))
    s = jnp.where(np.tril(np.ones((CHUNK, CHUNK), bool)), s, 0.0)
    o_intra = jnp.einsum('bnhts,bnshv->bnthv', s, v)
    b_last = b[:, :, -1]
    q_dec = q * jnp.exp(b)
    k_dec = k * jnp.exp(b_last[:, :, None] - b)

    def step(state, xs):
        qd, kd, vc, bl = xs
        o = jnp.einsum('bthk,bhkv->bthv', qd, state)
        state = jnp.exp(bl)[..., None] * state + jnp.einsum('bthk,bthv->bhkv', kd, vc)
        return state, o

    xs = (jnp.moveaxis(q_dec, 1, 0), jnp.moveaxis(k_dec, 1, 0), jnp.moveaxis(v, 1, 0), jnp.moveaxis(b_last, 1, 0))
    _, o_inter = lax.scan(step, jnp.zeros((Bsz, H, K, V), jnp.float32), xs)
    return (o_intra + jnp.moveaxis(o_inter, 0, 1)).reshape(Bsz, S, H, V)


def bidirectional_gated_scan(q, k_fwd, g_fwd, k_bwd, g_bwd, v):
    fwd = _chunk_gated_scan(q, k_fwd, v, g_fwd)
    flip = lambda t: jnp.flip(t, axis=1)
    bwd = flip(_chunk_gated_scan(flip(q), flip(k_bwd), flip(v), flip(g_bwd)))
    diag = jnp.sum(q * k_bwd, axis=-1, keepdims=True) * v
    return fwd + bwd - diag


def _head_rmsnorm(o, gain):
    o = o * lax.rsqrt(jnp.mean(o * o, axis=-1, keepdims=True) + NORM_EPS)
    return o.reshape(o.shape[0], o.shape[1], -1) * gain.astype(jnp.float32)


def _layer_norm(z, g, b):
    z32 = z.astype(jnp.float32)
    mu = jnp.mean(z32, axis=-1, keepdims=True)
    var = jnp.mean(jnp.square(z32 - mu), axis=-1, keepdims=True)
    out = (z32 - mu) * lax.rsqrt(var + NORM_EPS) * g.astype(jnp.float32) + b.astype(jnp.float32)
    return out.astype(z.dtype)


def _mixer_ab(h, w_in, gate_up, gate_bias, norm_gain, w_out, rel_bias):
    Bsz, S, _ = h.shape
    proj = jnp.einsum('bsd,df->bsf', h, w_in).astype(jnp.float32)
    qa, ka, va, qb, kb, vb, lr_f, lr_b, gate = _split(
        proj, (A_WIDTH, A_WIDTH, A_WIDTH, B_K_WIDTH, B_K_WIDTH, B_V_WIDTH, B_GATE_RANK, B_GATE_RANK, MIX_WIDTH))
    heads = lambda t, n: t.reshape(Bsz, S, n, -1)
    oa = dilated_attention(heads(qa, A_HEADS), heads(ka, A_HEADS), heads(va, A_HEADS), rel_bias)
    oa = oa.reshape(Bsz, S, A_WIDTH)
    up = gate_up.astype(jnp.float32)
    gb_ = gate_bias.astype(jnp.float32)
    g_f = jax.nn.log_sigmoid(lr_f @ up[0] + gb_[0]) / B_GATE_NORMALIZER
    g_b = jax.nn.log_sigmoid(lr_b @ up[1] + gb_[1]) / B_GATE_NORMALIZER
    qh = heads(qb, B_HEADS) * (B_KEY_DIM ** -0.5)
    kh = heads(kb, B_HEADS)
    ob = bidirectional_gated_scan(qh, kh, heads(g_f, B_HEADS), kh, heads(g_b, B_HEADS), heads(vb, B_HEADS))
    ob = _head_rmsnorm(ob, norm_gain)
    y = jnp.concatenate([oa, ob], axis=-1) * jax.nn.silu(gate)
    return jnp.einsum('bsf,fd->bsd', y.astype(h.dtype), w_out)


def _mixer_c(h, w_in, lower_bounds, layer_idx, norm_gain, w_out):
    Bsz, S, _ = h.shape
    proj = jnp.einsum('bsd,df->bsf', h, w_in).astype(jnp.float32)
    q, f_f, f_b, i, gate = _split(proj, (C_WIDTH,) * 5)
    lb = jax.nn.softmax(lower_bounds.astype(jnp.float32), axis=0)
    lb = (jnp.cumsum(lb, axis=0) - lb[0])[layer_idx]
    heads = lambda t: t.reshape(Bsz, S, C_HEADS, C_HEAD_DIM)

    def forget(z):
        f = lb + (1.0 - lb) * jax.nn.sigmoid(z)
        return heads(1.0 - f), heads(jnp.log(f))

    k_f, g_f = forget(f_f)
    k_b, g_b = forget(f_b)
    qh = heads(jax.nn.silu(q)) * (C_HEAD_DIM ** -0.5)
    o = bidirectional_gated_scan(qh, k_f, g_f, k_b, g_b, heads(i))
    y = _head_rmsnorm(o, norm_gain) * jax.nn.silu(gate)
    return jnp.einsum('bsf,fd->bsd', y.astype(h.dtype), w_out)


def setup_inputs(seed: int = 0) -> dict:
    key = jax.random.key(seed)
    ks = jax.random.split(key, 13)
    f32 = jnp.float32
    nrm = lambda k, shape, s: jax.random.normal(k, shape, f32) * s
    return {
        'x': nrm(ks[0], (BATCH, SEQ, D_MODEL), 1.0),
        'w_in_ab': nrm(ks[1], (N_EVEN, D_MODEL, AB_IN_WIDTH), D_MODEL ** -0.5),
        'gla_gate_up': nrm(ks[2], (N_EVEN, 2, B_GATE_RANK, B_K_WIDTH), B_GATE_RANK ** -0.5),
        'gla_gate_bias': nrm(ks[3], (N_EVEN, 2, B_K_WIDTH), 0.1),
        'gla_norm': 1.0 + nrm(ks[4], (N_EVEN, B_V_WIDTH), 0.02),
        'w_out_ab': nrm(ks[5], (N_EVEN, MIX_WIDTH, D_MODEL), MIX_WIDTH ** -0.5 * DEEPNORM_BETA),
        'w_in_c': nrm(ks[6], (N_ODD, D_MODEL, C_IN_WIDTH), D_MODEL ** -0.5),
        'hgrn_lower_bounds': nrm(ks[7], (DEPTH, C_WIDTH), 0.1),
        'hgrn_norm': 1.0 + nrm(ks[8], (N_ODD, C_WIDTH), 0.02),
        'w_out_c': nrm(ks[9], (N_ODD, C_WIDTH, D_MODEL), C_WIDTH ** -0.5 * DEEPNORM_BETA),
        'rel_bias': nrm(ks[10], (REL_BUCKETS, A_HEADS), 0.1),
        'ln_gain': 1.0 + nrm(ks[11], (DEPTH, D_MODEL), 0.02),
        'ln_bias': nrm(ks[12], (DEPTH, D_MODEL), 0.02),
    }


def reference(x, w_in_ab, gla_gate_up, gla_gate_bias, gla_norm, w_out_ab, w_in_c,
              hgrn_lower_bounds, hgrn_norm, w_out_c, rel_bias, ln_gain, ln_bias):
    for layer in range(DEPTH):
        if layer % 2 == 0:
            e = layer // 2
            y = _mixer_ab(x, w_in_ab[e], gla_gate_up[e], gla_gate_bias[e], gla_norm[e], w_out_ab[e], rel_bias)
        else:
            o = layer // 2
            y = _mixer_c(x, w_in_c[o], hgrn_lower_bounds, layer, hgrn_norm[o], w_out_c[o])
        x = _layer_norm(DEEPNORM_ALPHA * x + y, ln_gain[layer], ln_bias[layer])
    return x
```

```python
import functools

import numpy as np
import jax
import jax.numpy as jnp
from jax import lax
from jax.experimental import pallas as pl
from jax.experimental.pallas import tpu as pltpu

F32 = jnp.float32
BF16 = jnp.bfloat16

LANES = 128
VMEM_LIMIT = 56 * 1024 * 1024

D_MODEL = 1024
DEPTH = 4
A_HEADS = 8
A_HEAD_DIM = 64
A_WIDTH = A_HEADS * A_HEAD_DIM
A_GROUPS = A_WIDTH // LANES
DILATIONS = (1, 4, 16)
A_HALF_STEPS = 64
REL_BUCKETS = 32
REL_MAX_DISTANCE = 1024
B_HEADS = 4
B_KEY_DIM = 128
B_VAL_DIM = 256
B_K_WIDTH = B_HEADS * B_KEY_DIM
B_V_WIDTH = B_HEADS * B_VAL_DIM
B_GATE_RANK = 16
B_GATE_NORMALIZER = 16.0
C_HEADS = 12
C_HEAD_DIM = 128
C_WIDTH = C_HEADS * C_HEAD_DIM
MIX_WIDTH = A_WIDTH + B_V_WIDTH
MIX_GROUPS = MIX_WIDTH // LANES
NORM_EPS = 1e-5
DEEPNORM_ALPHA = (2 * DEPTH) ** 0.25

NEG = -1e30

CHUNK = 64
SUB = 32
ATT_Q = 128
ATT_K = ATT_Q + 2 * A_HALF_STEPS
ATT_TOKENS = 2048
ROW_BLOCK = 256
PROJ_ROWS = 512


def _params(sem):
    return pltpu.CompilerParams(dimension_semantics=sem, vmem_limit_bytes=VMEM_LIMIT)


def _dot(a, b):
    return jnp.dot(a, b, preferred_element_type=F32)


def _dot_nt(a, b):
    return lax.dot_general(a, b, (((1,), (1,)), ((), ())), preferred_element_type=F32)


def _dot_tn(a, b):
    return lax.dot_general(a, b, (((0,), (0,)), ((), ())), preferred_element_type=F32)


def _split2(x):
    hi = x.astype(BF16)
    lo = (x - hi.astype(F32)).astype(BF16)
    return hi, lo


def _dot_sel(m01, x):
    hi, lo = _split2(x)
    return _dot(m01, hi) + _dot(m01, lo)


def _dot_f32(a, b):
    ah, al = _split2(a)
    bh, bl = _split2(b)
    return _dot(ah, bh) + _dot(ah, bl) + _dot(al, bh)


def _sigmoid(z):
    return 1.0 / (1.0 + jnp.exp(-z))


def _silu(z):
    return z * _sigmoid(z)


def _log_sigmoid(z):
    return jnp.minimum(z, 0.0) - jnp.log(1.0 + jnp.exp(-jnp.abs(z)))


def _layer_norm(z, g, b):
    mu = jnp.mean(z, axis=-1, keepdims=True)
    zc = z - mu
    var = jnp.mean(zc * zc, axis=-1, keepdims=True)
    return zc * lax.rsqrt(var + NORM_EPS) * g + b


def _store_groups(acc, out_ref, g0):
    for k in range(acc.shape[1] // LANES):
        out_ref[g0 + k] = acc[:, k * LANES:(k + 1) * LANES].astype(out_ref.dtype)


def _inproj_ab_kernel(x_ref, w_ref, wlr_ref, aqkv_ref, bqk_ref, bv_ref, gate_ref, lr_ref):
    xb = x_ref[...].astype(BF16)
    step = 4 * LANES
    for c in range(0, 3 * A_WIDTH, step):
        _store_groups(_dot(xb, w_ref[:, c:c + step]), aqkv_ref, c // LANES)
    base = 3 * A_WIDTH
    for c in range(0, 2 * B_K_WIDTH, step):
        acc = _dot(xb, w_ref[:, base + c:base + c + step])
        if c < B_K_WIDTH:
            acc = acc * (B_KEY_DIM ** -0.5)
        _store_groups(acc, bqk_ref, c // LANES)
    base += 2 * B_K_WIDTH
    for c in range(0, B_V_WIDTH, step):
        _store_groups(_dot(xb, w_ref[:, base + c:base + c + step]), bv_ref, c // LANES)
    base += B_V_WIDTH
    for c in range(0, MIX_WIDTH, step):
        _store_groups(_dot(xb, w_ref[:, base + c:base + c + step]), gate_ref, c // LANES)
    lr_ref[...] = _dot(xb, wlr_ref[...])


def _inproj_ab(x2d, w_main, w_lr):
    T = x2d.shape[0]
    tm = PROJ_ROWS
    n_main = w_main.shape[1]
    grp = lambda n: pl.BlockSpec((n, tm, LANES), lambda i: (0, i, 0))
    return pl.pallas_call(
        _inproj_ab_kernel,
        grid=(T // tm,),
        in_specs=[
            pl.BlockSpec((tm, D_MODEL), lambda i: (i, 0)),
            pl.BlockSpec((D_MODEL, n_main), lambda i: (0, 0)),
            pl.BlockSpec((D_MODEL, LANES), lambda i: (0, 0)),
        ],
        out_specs=[grp(3 * A_GROUPS), grp(8), grp(8), grp(MIX_GROUPS),
                   pl.BlockSpec((tm, LANES), lambda i: (i, 0))],
        out_shape=[
            jax.ShapeDtypeStruct((3 * A_GROUPS, T, LANES), BF16),
            jax.ShapeDtypeStruct((8, T, LANES), BF16),
            jax.ShapeDtypeStruct((8, T, LANES), BF16),
            jax.ShapeDtypeStruct((MIX_GROUPS, T, LANES), BF16),
            jax.ShapeDtypeStruct((T, LANES), F32),
        ],
        compiler_params=_params(("parallel",)),
        name="inproj_ab",
    )(x2d, w_main, w_lr)


def _inproj_c_kernel(x_ref, w_ref, lbnd_ref, q_ref, i_ref, gate_ref, lg_ref, *, layer_idx):
    xb = x_ref[...].astype(BF16)
    lbnd = lbnd_ref[...]
    e = jnp.exp(lbnd - jnp.max(lbnd, axis=0, keepdims=True))
    sm = e / jnp.sum(e, axis=0, keepdims=True)
    lb = jnp.sum(sm[1:layer_idx + 1], axis=0, keepdims=True)
    step = 4 * LANES
    for c in range(0, C_WIDTH, step):
        q = _dot(xb, w_ref[:, c:c + step])
        _store_groups(_silu(q) * (C_HEAD_DIM ** -0.5), q_ref, c // LANES)
    for d in range(2):
        base = (1 + d) * C_WIDTH
        for c in range(0, C_WIDTH, step):
            z = _dot(xb, w_ref[:, base + c:base + c + step])
            lbc = lb[:, c:c + step]
            f = lbc + (1.0 - lbc) * _sigmoid(z)
            _store_groups(jnp.log(f), lg_ref, (d * C_WIDTH + c) // LANES)
    for c in range(0, C_WIDTH, step):
        _store_groups(_dot(xb, w_ref[:, 3 * C_WIDTH + c:3 * C_WIDTH + c + step]), i_ref, c // LANES)
    for c in range(0, C_WIDTH, step):
        _store_groups(_dot(xb, w_ref[:, 4 * C_WIDTH + c:4 * C_WIDTH + c + step]), gate_ref, c // LANES)


def _inproj_c(x2d, w, lower_bounds, layer_idx):
    T = x2d.shape[0]
    tm = PROJ_ROWS // 2
    grp = lambda n: pl.BlockSpec((n, tm, LANES), lambda i: (0, i, 0))
    return pl.pallas_call(
        functools.partial(_inproj_c_kernel, layer_idx=layer_idx),
        grid=(T // tm,),
        in_specs=[
            pl.BlockSpec((tm, D_MODEL), lambda i: (i, 0)),
            pl.BlockSpec((D_MODEL, 5 * C_WIDTH), lambda i: (0, 0)),
            pl.BlockSpec((DEPTH, C_WIDTH), lambda i: (0, 0)),
        ],
        out_specs=[grp(C_HEADS), grp(C_HEADS), grp(C_HEADS), grp(2 * C_HEADS)],
        out_shape=[
            jax.ShapeDtypeStruct((C_HEADS, T, LANES), BF16),
            jax.ShapeDtypeStruct((C_HEADS, T, LANES), BF16),
            jax.ShapeDtypeStruct((C_HEADS, T, LANES), BF16),
            jax.ShapeDtypeStruct((2 * C_HEADS, T, LANES), F32),
        ],
        compiler_params=_params(("parallel",)),
        name="inproj_c",
    )(x2d, w, lower_bounds)


def _t5_bucket(rel):
    half = REL_BUCKETS // 2
    max_exact = half // 2
    n = np.abs(rel)
    large = max_exact + (np.log(np.maximum(n, 1) / max_exact)
                         / np.log(REL_MAX_DISTANCE / max_exact) * (half - max_exact)).astype(np.int32)
    large = np.minimum(large, half - 1)
    return np.where(rel > 0, half, 0) + np.where(n < max_exact, n, large)


def _bias_table(rel_bias, dilation):
    qi = np.arange(ATT_Q)[:, None]
    kj = np.arange(ATT_K)[None, :]
    off = kj - A_HALF_STEPS - qi
    band = np.abs(off) <= A_HALF_STEPS
    tbl = rel_bias.astype(F32)[_t5_bucket(off * dilation)]
    tbl = jnp.where(band[:, :, None], tbl, NEG)
    tbl = jnp.transpose(tbl, (2, 0, 1)).reshape(A_GROUPS, 2, ATT_Q, ATT_K)
    return jnp.transpose(tbl, (0, 2, 1, 3)).reshape(A_GROUPS, ATT_Q, 2 * ATT_K)


def _attn_kernel(q_ref, kp_ref, kc_ref, kn_ref, vp_ref, vc_ref, vn_ref, bias_ref,
                 o_ref, lse_ref, kext, vext, *, dilation, rows, seq_rows):
    j = pl.program_id(1)
    halo = A_HALF_STEPS
    kext[:, 0:halo, :] = kp_ref[...]
    kext[:, halo:halo + rows, :] = kc_ref[...]
    kext[:, halo + rows:, :] = kn_ref[...]
    vext[:, 0:halo, :] = vp_ref[...]
    vext[:, halo:halo + rows, :] = vc_ref[...]
    vext[:, halo + rows:, :] = vn_ref[...]

    low_kv = lax.broadcasted_iota(jnp.int32, (ATT_K, LANES), 1) < A_HEAD_DIM
    low_q = lax.broadcasted_iota(jnp.int32, (ATT_Q, LANES), 1) < A_HEAD_DIM
    key_in_blk = lax.broadcasted_iota(jnp.int32, (1, 2 * ATT_K), 1) & (ATT_K - 1)

    def group_body(g, carry):
        bias = bias_ref[g]
        for r in range(dilation):
            cols = slice(r * LANES, (r + 1) * LANES)
            for i in range(rows // ATT_Q):
                qrows = slice(i * ATT_Q, (i + 1) * ATT_Q)
                krows = slice(i * ATT_Q, i * ATT_Q + ATT_K)
                q2 = q_ref[g, qrows, cols] * jnp.asarray(A_HEAD_DIM ** -0.5, BF16)
                k2 = kext[g, krows, cols]
                v2 = vext[g, krows, cols]
                zero = jnp.zeros_like(k2)
                kpad = jnp.concatenate([jnp.where(low_kv, k2, zero), jnp.where(low_kv, zero, k2)], axis=0)
                vpad = jnp.concatenate([jnp.where(low_kv, v2, zero), jnp.where(low_kv, zero, v2)], axis=0)
                key_row = key_in_blk + (j * rows + (i * ATT_Q - halo))
                edge = jnp.where((key_row >= 0) & (key_row < seq_rows), 0.0, NEG)
                s = _dot_nt(q2, kpad) + bias + edge
                s0, s1 = s[:, :ATT_K], s[:, ATT_K:]
                m0 = jnp.max(s0, axis=1, keepdims=True)
                m1 = jnp.max(s1, axis=1, keepdims=True)
                p0 = jnp.exp(s0 - m0)
                p1 = jnp.exp(s1 - m1)
                l0 = jnp.sum(p0, axis=1, keepdims=True)
                l1 = jnp.sum(p1, axis=1, keepdims=True)
                p = jnp.concatenate([p0, p1], axis=1).astype(BF16)
                pv = _dot(p, vpad)
                o_ref[g, qrows, cols] = (pv * jnp.where(low_q, 1.0 / l0, 1.0 / l1)).astype(o_ref.dtype)
                lse_ref[g, qrows, cols] = jnp.where(low_q, m0 + jnp.log(l0), m1 + jnp.log(l1))
        return carry

    lax.fori_loop(0, A_GROUPS, group_body, 0)


def _attention_pattern(aqkv, bias_tbl, batch, seq, dilation):
    T = aqkv.shape[1]
    L = seq // dilation
    rows = ATT_TOKENS // dilation
    width = dilation * LANES
    halo = A_HALF_STEPS
    hb = rows // halo
    view = aqkv.reshape(3 * A_GROUPS, batch, L, width)
    cur = lambda part: pl.BlockSpec((A_GROUPS, None, rows, width), lambda b, j: (part, b, j, 0))
    prev = lambda part: pl.BlockSpec((A_GROUPS, None, halo, width),
                                     lambda b, j: (part, b, jnp.maximum(j * hb - 1, 0), 0))
    nxt = lambda part: pl.BlockSpec((A_GROUPS, None, halo, width),
                                    lambda b, j: (part, b, jnp.minimum((j + 1) * hb, L // halo - 1), 0))
    out_spec = pl.BlockSpec((A_GROUPS, None, rows, width), lambda b, j: (0, b, j, 0))
    o, lse = pl.pallas_call(
        functools.partial(_attn_kernel, dilation=dilation, rows=rows, seq_rows=L),
        grid=(batch, L // rows),
        in_specs=[cur(0), prev(1), cur(1), nxt(1), prev(2), cur(2), nxt(2),
                  pl.BlockSpec((A_GROUPS, ATT_Q, 2 * ATT_K), lambda b, j: (0, 0, 0))],
        out_specs=[out_spec, out_spec],
        out_shape=[jax.ShapeDtypeStruct((A_GROUPS, batch, L, width), BF16),
                   jax.ShapeDtypeStruct((A_GROUPS, batch, L, width), F32)],
        scratch_shapes=[pltpu.VMEM((A_GROUPS, rows + 2 * halo, width), BF16),
                        pltpu.VMEM((A_GROUPS, rows + 2 * halo, width), BF16)],
        compiler_params=_params(("parallel", "parallel")),
        name=f"dilated_attn_d{dilation}",
    )(view, view, view, view, view, view, view, bias_tbl)
    return o.reshape(A_GROUPS, T, LANES), lse.reshape(A_GROUPS, T, LANES)


def _scan_constants():
    t = np.arange(CHUNK)[:, None]
    r = np.arange(CHUNK)[None, :]
    blk = t // SUB
    fwd = np.concatenate([r <= t, r <= blk * SUB + SUB // 2 - 1, r < blk * SUB, r < (blk + 1) * SUB], axis=0)
    bwd = np.concatenate([r >= t, r >= blk * SUB + SUB // 2, r >= (blk + 1) * SUB, r >= blk * SUB], axis=0)
    ones = np.ones((8, CHUNK), bool)
    state_f = np.concatenate([r > t, ones], axis=0)
    state_b = np.concatenate([r < t, ones], axis=0)
    as_bf16 = lambda m: jnp.asarray(m.astype(np.float32), BF16)
    return as_bf16(fwd), as_bf16(bwd), as_bf16(state_f), as_bf16(state_b)


def _intra_masks():
    t = lax.broadcasted_iota(jnp.int32, (CHUNK, CHUNK), 0)
    s = lax.broadcasted_iota(jnp.int32, (CHUNK, CHUNK), 1)
    tb, sb = t // SUB, s // SUB
    return (tb == sb) & (s <= t), tb > sb, (tb == sb) & (s > t), tb < sb


def _chunk_output(q, kf, kb, lgf, lgb, v, stf, stb, mf, mb, masks):
    m_df, m_of, m_db, m_ob = masks
    c = CHUNK
    cf = _dot_sel(mf, lgf)
    p, p_mid, p_start, p_end = cf[0:c], cf[c:2 * c], cf[2 * c:3 * c], cf[3 * c:4 * c]
    cb = _dot_sel(mb, lgb)
    s, s_mid, s_next, s_own = cb[0:c], cb[c:2 * c], cb[2 * c:3 * c], cb[3 * c:4 * c]
    bf = lambda a: a.astype(BF16)
    a = jnp.where(m_df, _dot_nt(bf(q * jnp.exp(p - p_mid)), bf(kf * jnp.exp(p_mid - p))), 0.0)
    a = a + jnp.where(m_of, _dot_nt(bf(q * jnp.exp(p - p_start)), bf(kf * jnp.exp(p_end - p))), 0.0)
    a = a + jnp.where(m_db, _dot_nt(bf(q * jnp.exp(s - s_mid)), bf(kb * jnp.exp(s_mid - s))), 0.0)
    a = a + jnp.where(m_ob, _dot_nt(bf(q * jnp.exp(s - s_next)), bf(kb * jnp.exp(s_own - s))), 0.0)
    return _dot(bf(a), v) + _dot_nt(bf(q * jnp.exp(p)), stf) + _dot_nt(bf(q * jnp.exp(s)), stb)


def _state_step(st_ref, out_ref, h, ci, k, lg, v, msel):
    c = CHUNK
    d = _dot_sel(msel, lg)
    decay, total = d[0:c], d[c:c + 1]
    kdec = (k * jnp.exp(decay)).astype(BF16)
    st = st_ref[h]
    out_ref[ci, h] = st.astype(out_ref.dtype)
    st_ref[h] = st * jnp.exp(total) + _dot_tn(v, kdec)


def _head_value(v_ref, h, rows, nv):
    if nv == 1:
        return v_ref[h, rows, :]
    return jnp.concatenate([v_ref[nv * h + k, rows, :] for k in range(nv)], axis=1)


def _gla_log_gates(lr, up_pad, bias_pad, lg_scr):
    z = _dot_f32(lr, up_pad) + bias_pad
    lg = _log_sigmoid(z) * (1.0 / B_GATE_NORMALIZER)
    for g in range(2 * B_HEADS):
        lg_scr[g] = lg[:, g * LANES:(g + 1) * LANES]


def _state_kernel_c(lg_ref, v_ref, msel_ref, out_ref, st_ref, *, reverse):
    @pl.when(pl.program_id(1) == 0)
    def _():
        st_ref[...] = jnp.zeros_like(st_ref)

    msel = msel_ref[...]
    n_chunks = ROW_BLOCK // CHUNK
    order = range(n_chunks - 1, -1, -1) if reverse else range(n_chunks)

    def head_body(h, carry):
        for ci in order:
            rows = slice(ci * CHUNK, (ci + 1) * CHUNK)
            lg = lg_ref[h, rows, :]
            _state_step(st_ref, out_ref, h, ci, 1.0 - jnp.exp(lg), lg, v_ref[h, rows, :], msel)
        return carry

    lax.fori_loop(0, C_HEADS, head_body, 0)


def _state_kernel_b(lr_ref, k_ref, v_ref, up_ref, bias_ref, msel_ref, out_ref, st_ref, lg_scr, *, reverse):
    @pl.when(pl.program_id(1) == 0)
    def _():
        st_ref[...] = jnp.zeros_like(st_ref)

    _gla_log_gates(lr_ref[...], up_ref[...], bias_ref[...], lg_scr)
    msel = msel_ref[...]
    n_chunks = ROW_BLOCK // CHUNK
    order = range(n_chunks - 1, -1, -1) if reverse else range(n_chunks)
    gate0 = B_HEADS if reverse else 0

    def head_body(h, carry):
        for ci in order:
            rows = slice(ci * CHUNK, (ci + 1) * CHUNK)
            v = _head_value(v_ref, h, rows, B_VAL_DIM // LANES)
            _state_step(st_ref, out_ref, h, ci, k_ref[h, rows, :].astype(F32), lg_scr[gate0 + h, rows, :],
                        v, msel)
        return carry

    lax.fori_loop(0, B_HEADS, head_body, 0)


def _seq_block_index(reverse, blocks_per_seq):
    if reverse:
        return lambda b, n: b * blocks_per_seq + (blocks_per_seq - 1 - n)
    return lambda b, n: b * blocks_per_seq + n


def _states_c(lg, v, msel, batch, seq, reverse):
    T = v.shape[1]
    nb = seq // ROW_BLOCK
    cpb = ROW_BLOCK // CHUNK
    blk = _seq_block_index(reverse, nb)
    gate_part = 1 if reverse else 0
    return pl.pallas_call(
        functools.partial(_state_kernel_c, reverse=reverse),
        grid=(batch, nb),
        in_specs=[
            pl.BlockSpec((C_HEADS, ROW_BLOCK, LANES), lambda b, n: (gate_part, blk(b, n), 0)),
            pl.BlockSpec((C_HEADS, ROW_BLOCK, LANES), lambda b, n: (0, blk(b, n), 0)),
            pl.BlockSpec(msel.shape, lambda b, n: (0, 0)),
        ],
        out_specs=pl.BlockSpec((cpb, C_HEADS, C_HEAD_DIM, C_HEAD_DIM), lambda b, n: (blk(b, n), 0, 0, 0)),
        out_shape=jax.ShapeDtypeStruct((T // CHUNK, C_HEADS, C_HEAD_DIM, C_HEAD_DIM), BF16),
        scratch_shapes=[pltpu.VMEM((C_HEADS, C_HEAD_DIM, C_HEAD_DIM), F32)],
        compiler_params=_params(("parallel", "arbitrary")),
        name="hgrn_states_bwd" if reverse else "hgrn_states_fwd",
    )(lg, v, msel)


def _states_b(lr, bqk, bv, up_pad, bias_pad, msel, batch, seq, reverse):
    T = lr.shape[0]
    nb = seq // ROW_BLOCK
    cpb = ROW_BLOCK // CHUNK
    blk = _seq_block_index(reverse, nb)
    full = lambda a: pl.BlockSpec(a.shape, lambda b, n: (0,) * a.ndim)
    return pl.pallas_call(
        functools.partial(_state_kernel_b, reverse=reverse),
        grid=(batch, nb),
        in_specs=[
            pl.BlockSpec((ROW_BLOCK, LANES), lambda b, n: (blk(b, n), 0)),
            pl.BlockSpec((B_HEADS, ROW_BLOCK, LANES), lambda b, n: (1, blk(b, n), 0)),
            pl.BlockSpec((2 * B_HEADS, ROW_BLOCK, LANES), lambda b, n: (0, blk(b, n), 0)),
            full(up_pad), full(bias_pad), full(msel),
        ],
        out_specs=pl.BlockSpec((cpb, B_HEADS, B_VAL_DIM, B_KEY_DIM), lambda b, n: (blk(b, n), 0, 0, 0)),
        out_shape=jax.ShapeDtypeStruct((T // CHUNK, B_HEADS, B_VAL_DIM, B_KEY_DIM), BF16),
        scratch_shapes=[pltpu.VMEM((B_HEADS, B_VAL_DIM, B_KEY_DIM), F32),
                        pltpu.VMEM((2 * B_HEADS, ROW_BLOCK, LANES), F32)],
        compiler_params=_params(("parallel", "arbitrary")),
        name="gla_states_bwd" if reverse else "gla_states_fwd",
    )(lr, bqk, bv, up_pad, bias_pad, msel)


def _project_and_norm(y_scr, wout_ref, x_ref, lng_ref, lnb_ref, out_ref):
    y = jnp.concatenate([y_scr[g] for g in range(MIX_GROUPS)], axis=1)
    z = DEEPNORM_ALPHA * x_ref[...] + _dot(y, wout_ref[...])
    out_ref[...] = _layer_norm(z, lng_ref[...], lnb_ref[...])


def _rms_gain(o, gain):
    return o * lax.rsqrt(jnp.mean(o * o, axis=-1, keepdims=True) + NORM_EPS) * gain


def _out_kernel_c(q_ref, i_ref, gate_ref, lgf_ref, lgb_ref, sf_ref, sb_ref, mf_ref, mb_ref, gain_ref,
                  wout_ref, x_ref, lng_ref, lnb_ref, out_ref, y_scr):
    mf, mb = mf_ref[...], mb_ref[...]
    masks = _intra_masks()

    def head_body(h, carry):
        for ci in range(ROW_BLOCK // CHUNK):
            rows = slice(ci * CHUNK, (ci + 1) * CHUNK)
            lgf = lgf_ref[h, rows, :]
            lgb = lgb_ref[h, rows, :]
            o = _chunk_output(q_ref[h, rows, :].astype(F32), 1.0 - jnp.exp(lgf), 1.0 - jnp.exp(lgb),
                              lgf, lgb, i_ref[h, rows, :], sf_ref[ci, h], sb_ref[ci, h], mf, mb, masks)
            y = _rms_gain(o, gain_ref[h]) * _silu(gate_ref[h, rows, :].astype(F32))
            y_scr[h, rows, :] = y.astype(BF16)
        return carry

    lax.fori_loop(0, C_HEADS, head_body, 0)
    _project_and_norm(y_scr, wout_ref, x_ref, lng_ref, lnb_ref, out_ref)


def _out_kernel_ab(o1_ref, o2_ref, o3_ref, l1_ref, l2_ref, l3_ref, qk_ref, v_ref, gate_ref, lr_ref,
                   sf_ref, sb_ref, mf_ref, mb_ref, up_ref, gbias_ref, gain_ref,
                   wout_ref, x_ref, lng_ref, lnb_ref, out_ref, y_scr, lg_scr):
    for g in range(A_GROUPS):
        l1, l2, l3 = l1_ref[g], l2_ref[g], l3_ref[g]
        m = jnp.maximum(jnp.maximum(l1, l2), l3)
        e1, e2, e3 = jnp.exp(l1 - m), jnp.exp(l2 - m), jnp.exp(l3 - m)
        num = e1 * o1_ref[g].astype(F32) + e2 * o2_ref[g].astype(F32) + e3 * o3_ref[g].astype(F32)
        oa = num / (e1 + e2 + e3)
        y_scr[g] = (oa * _silu(gate_ref[g].astype(F32))).astype(BF16)

    _gla_log_gates(lr_ref[...], up_ref[...], gbias_ref[...], lg_scr)
    mf, mb = mf_ref[...], mb_ref[...]
    masks = _intra_masks()
    nv = B_VAL_DIM // LANES

    def head_body(h, carry):
        gain = jnp.concatenate([gain_ref[nv * h + k] for k in range(nv)], axis=1)
        for ci in range(ROW_BLOCK // CHUNK):
            rows = slice(ci * CHUNK, (ci + 1) * CHUNK)
            k = qk_ref[B_HEADS + h, rows, :].astype(F32)
            o = _chunk_output(qk_ref[h, rows, :].astype(F32), k, k,
                              lg_scr[h, rows, :], lg_scr[B_HEADS + h, rows, :],
                              _head_value(v_ref, h, rows, nv), sf_ref[ci, h], sb_ref[ci, h], mf, mb, masks)
            o = _rms_gain(o, gain)
            for kk in range(nv):
                grp = A_GROUPS + nv * h + kk
                y = o[:, kk * LANES:(kk + 1) * LANES] * _silu(gate_ref[grp, rows, :].astype(F32))
                y_scr[grp, rows, :] = y.astype(BF16)
        return carry

    lax.fori_loop(0, B_HEADS, head_body, 0)
    _project_and_norm(y_scr, wout_ref, x_ref, lng_ref, lnb_ref, out_ref)


def _row_specs(tb):
    grp = lambda n, part=0: pl.BlockSpec((n, tb, LANES), lambda i: (part, i, 0))
    full = lambda a: pl.BlockSpec(a.shape, lambda i: (0,) * a.ndim)
    rows = lambda width: pl.BlockSpec((tb, width), lambda i: (i, 0))
    return grp, full, rows


def _output_c(cq, ci_, gate, lg, sf, sb, mf, mb, gain, wout, x2d, lng, lnb):
    T = x2d.shape[0]
    tb = ROW_BLOCK
    cpb = tb // CHUNK
    grp, full, rows = _row_specs(tb)
    st = pl.BlockSpec((cpb, C_HEADS, C_HEAD_DIM, C_HEAD_DIM), lambda i: (i, 0, 0, 0))
    return pl.pallas_call(
        _out_kernel_c,
        grid=(T // tb,),
        in_specs=[grp(C_HEADS), grp(C_HEADS), grp(C_HEADS), grp(C_HEADS, 0), grp(C_HEADS, 1), st, st,
                  full(mf), full(mb), full(gain), full(wout), rows(D_MODEL), full(lng), full(lnb)],
        out_specs=rows(D_MODEL),
        out_shape=jax.ShapeDtypeStruct((T, D_MODEL), F32),
        scratch_shapes=[pltpu.VMEM((MIX_GROUPS, tb, LANES), BF16)],
        compiler_params=_params(("parallel",)),
        name="hgrn_output",
    )(cq, ci_, gate, lg, lg, sf, sb, mf, mb, gain, wout, x2d, lng, lnb)


def _output_ab(attn, bqk, bv, gate, lr, sf, sb, mf, mb, up_pad, gbias_pad, gain, wout, x2d, lng, lnb):
    T = x2d.shape[0]
    tb = ROW_BLOCK
    cpb = tb // CHUNK
    grp, full, rows = _row_specs(tb)
    st = pl.BlockSpec((cpb, B_HEADS, B_VAL_DIM, B_KEY_DIM), lambda i: (i, 0, 0, 0))
    (o1, l1), (o2, l2), (o3, l3) = attn
    return pl.pallas_call(
        _out_kernel_ab,
        grid=(T // tb,),
        in_specs=[grp(A_GROUPS)] * 6 + [grp(2 * B_HEADS), grp(8), grp(MIX_GROUPS), rows(LANES), st, st,
                                        full(mf), full(mb), full(up_pad), full(gbias_pad), full(gain),
                                        full(wout), rows(D_MODEL), full(lng), full(lnb)],
        out_specs=rows(D_MODEL),
        out_shape=jax.ShapeDtypeStruct((T, D_MODEL), F32),
        scratch_shapes=[pltpu.VMEM((MIX_GROUPS, tb, LANES), BF16),
                        pltpu.VMEM((2 * B_HEADS, tb, LANES), F32)],
        compiler_params=_params(("parallel",)),
        name="ab_output",
    )(o1, o2, o3, l1, l2, l3, bqk, bv, gate, lr, sf, sb, mf, mb, up_pad, gbias_pad, gain, wout, x2d, lng, lnb)


def _layer_ab(x2d, batch, seq, w_in, gate_up, gate_bias, norm_gain, w_out, rel_bias, lng, lnb, consts):
    mf, mb, msel_f, msel_b = consts
    sizes = (3 * A_WIDTH + 2 * B_K_WIDTH + B_V_WIDTH, 2 * B_GATE_RANK, MIX_WIDTH)
    o0, o1 = sizes[0], sizes[0] + sizes[1]
    w_main = jnp.concatenate([w_in[:, :o0], w_in[:, o1:]], axis=1).astype(BF16)
    w_lr = jnp.pad(w_in[:, o0:o1], ((0, 0), (0, LANES - sizes[1]))).astype(BF16)
    up = gate_up.astype(F32)
    up_pad = jnp.zeros((LANES, 2 * B_K_WIDTH), F32)
    up_pad = up_pad.at[0:B_GATE_RANK, 0:B_K_WIDTH].set(up[0])
    up_pad = up_pad.at[B_GATE_RANK:2 * B_GATE_RANK, B_K_WIDTH:].set(up[1])
    gbias_pad = gate_bias.astype(F32).reshape(1, 2 * B_K_WIDTH)

    aqkv, bqk, bv, gate, lr = _inproj_ab(x2d, w_main, w_lr)
    attn = [_attention_pattern(aqkv, _bias_table(rel_bias, d), batch, seq, d) for d in DILATIONS]
    sf = _states_b(lr, bqk, bv, up_pad, gbias_pad, msel_f, batch, seq, reverse=False)
    sb = _states_b(lr, bqk, bv, up_pad, gbias_pad, msel_b, batch, seq, reverse=True)
    gain = norm_gain.astype(F32).reshape(B_V_WIDTH // LANES, 1, LANES)
    wout = w_out.astype(BF16)
    return _output_ab(attn, bqk, bv, gate, lr, sf, sb, mf, mb, up_pad, gbias_pad, gain, wout, x2d, lng, lnb)


def _layer_c(x2d, batch, seq, w_in, lower_bounds, layer_idx, norm_gain, w_out, lng, lnb, consts):
    mf, mb, msel_f, msel_b = consts
    cq, ci_, gate, lg = _inproj_c(x2d, w_in.astype(BF16), lower_bounds.astype(F32), layer_idx)
    sf = _states_c(lg, ci_, msel_f, batch, seq, reverse=False)
    sb = _states_c(lg, ci_, msel_b, batch, seq, reverse=True)
    gain = norm_gain.astype(F32).reshape(C_HEADS, 1, LANES)
    wout = w_out.astype(BF16)
    return _output_c(cq, ci_, gate, lg, sf, sb, mf, mb, gain, wout, x2d, lng, lnb)


def kernel(x, w_in_ab, gla_gate_up, gla_gate_bias, gla_norm, w_out_ab, w_in_c, hgrn_lower_bounds, hgrn_norm,
           w_out_c, rel_bias, ln_gain, ln_bias):
    batch, seq, _ = x.shape
    assert seq % ATT_TOKENS == 0 and seq % ROW_BLOCK == 0 and (batch * seq) % PROJ_ROWS == 0
    consts = _scan_constants()
    x2d = x.astype(F32).reshape(batch * seq, D_MODEL)
    for layer in range(DEPTH):
        lng = ln_gain[layer].astype(F32).reshape(1, D_MODEL)
        lnb = ln_bias[layer].astype(F32).reshape(1, D_MODEL)
        if layer % 2 == 0:
            e = layer // 2
            x2d = _layer_ab(x2d, batch, seq, w_in_ab[e], gla_gate_up[e], gla_gate_bias[e], gla_norm[e],
                            w_out_ab[e], rel_bias, lng, lnb, consts)
        else:
            o = layer // 2
            x2d = _layer_c(x2d, batch, seq, w_in_c[o], hgrn_lower_bounds, layer, hgrn_norm[o], w_out_c[o],
                           lng, lnb, consts)
    return x2d.reshape(batch, seq, D_MODEL).astype(x.dtype)
```

```python
import functools

import numpy as np
import jax
import jax.numpy as jnp
from jax import lax
from jax.experimental import pallas as pl
from jax.experimental.pallas import tpu as pltpu

F32 = jnp.float32
BF16 = jnp.bfloat16

LANES = 128
VMEM_LIMIT = 56 * 1024 * 1024

D_MODEL = 1024
DEPTH = 4
A_HEADS = 8
A_HEAD_DIM = 64
A_WIDTH = A_HEADS * A_HEAD_DIM
A_GROUPS = A_WIDTH // LANES
DILATIONS = (1, 4, 16)
A_HALF_STEPS = 64
REL_BUCKETS = 32
REL_MAX_DISTANCE = 1024
B_HEADS = 4
B_KEY_DIM = 128
B_VAL_DIM = 256
B_K_WIDTH = B_HEADS * B_KEY_DIM
B_V_WIDTH = B_HEADS * B_VAL_DIM
B_GATE_RANK = 16
B_GATE_NORMALIZER = 16.0
C_HEADS = 12
C_HEAD_DIM = 128
C_WIDTH = C_HEADS * C_HEAD_DIM
MIX_WIDTH = A_WIDTH + B_V_WIDTH
MIX_GROUPS = MIX_WIDTH // LANES
NORM_EPS = 1e-5
DEEPNORM_ALPHA = (2 * DEPTH) ** 0.25

NEG = -1e30

CHUNK = 64
SUB = 32
ATT_Q = 128
ATT_K = ATT_Q + 2 * A_HALF_STEPS
ATT_TOKENS = 2048
ROW_BLOCK = 256
TOTAL_ROWS = 8
HEADS_PER_STEP = 2
PROJ_ROWS = 512


def _params(sem):
    return pltpu.CompilerParams(dimension_semantics=sem, vmem_limit_bytes=VMEM_LIMIT)


def _dot(a, b):
    return jnp.dot(a, b, preferred_element_type=F32)


def _dot_nt(a, b):
    return lax.dot_general(a, b, (((1,), (1,)), ((), ())), preferred_element_type=F32)


def _dot_tn(a, b):
    return lax.dot_general(a, b, (((0,), (0,)), ((), ())), preferred_element_type=F32)


def _split2(x):
    hi = x.astype(BF16)
    lo = (x - hi.astype(F32)).astype(BF16)
    return hi, lo


def _dot_sel(m01, x):
    hi, lo = _split2(x)
    return _dot(m01, hi) + _dot(m01, lo)


def _dot_f32(a, b):
    ah, al = _split2(a)
    bh, bl = _split2(b)
    return _dot(ah, bh) + _dot(ah, bl) + _dot(al, bh)


def _sigmoid(z):
    return 1.0 / (1.0 + jnp.exp(-z))


def _silu(z):
    return z * _sigmoid(z)


def _log_sigmoid(z):
    return jnp.minimum(z, 0.0) - jnp.log(1.0 + jnp.exp(-jnp.abs(z)))


def _layer_norm(z, g, b):
    mu = jnp.mean(z, axis=-1, keepdims=True)
    zc = z - mu
    var = jnp.mean(zc * zc, axis=-1, keepdims=True)
    return zc * lax.rsqrt(var + NORM_EPS) * g + b


def _store_groups(acc, out_ref, g0):
    for k in range(acc.shape[1] // LANES):
        out_ref[g0 + k] = acc[:, k * LANES:(k + 1) * LANES].astype(out_ref.dtype)


def _inproj_ab_kernel(x_ref, w_ref, wlr_ref, aqkv_ref, bqk_ref, bv_ref, gate_ref, lr_ref):
    xb = x_ref[...].astype(BF16)
    step = 4 * LANES
    for c in range(0, 3 * A_WIDTH, step):
        _store_groups(_dot(xb, w_ref[:, c:c + step]), aqkv_ref, c // LANES)
    base = 3 * A_WIDTH
    for c in range(0, 2 * B_K_WIDTH, step):
        acc = _dot(xb, w_ref[:, base + c:base + c + step])
        if c < B_K_WIDTH:
            acc = acc * (B_KEY_DIM ** -0.5)
        _store_groups(acc, bqk_ref, c // LANES)
    base += 2 * B_K_WIDTH
    for c in range(0, B_V_WIDTH, step):
        _store_groups(_dot(xb, w_ref[:, base + c:base + c + step]), bv_ref, c // LANES)
    base += B_V_WIDTH
    for c in range(0, MIX_WIDTH, step):
        _store_groups(_dot(xb, w_ref[:, base + c:base + c + step]), gate_ref, c // LANES)
    lr_ref[...] = _dot(xb, wlr_ref[...])


def _inproj_ab(x2d, w_main, w_lr):
    T = x2d.shape[0]
    tm = PROJ_ROWS
    n_main = w_main.shape[1]
    grp = lambda n: pl.BlockSpec((n, tm, LANES), lambda i: (0, i, 0))
    return pl.pallas_call(
        _inproj_ab_kernel,
        grid=(T // tm,),
        in_specs=[
            pl.BlockSpec((tm, D_MODEL), lambda i: (i, 0)),
            pl.BlockSpec((D_MODEL, n_main), lambda i: (0, 0)),
            pl.BlockSpec((D_MODEL, LANES), lambda i: (0, 0)),
        ],
        out_specs=[grp(3 * A_GROUPS), grp(8), grp(8), grp(MIX_GROUPS),
                   pl.BlockSpec((tm, LANES), lambda i: (i, 0))],
        out_shape=[
            jax.ShapeDtypeStruct((3 * A_GROUPS, T, LANES), BF16),
            jax.ShapeDtypeStruct((8, T, LANES), BF16),
            jax.ShapeDtypeStruct((8, T, LANES), BF16),
            jax.ShapeDtypeStruct((MIX_GROUPS, T, LANES), BF16),
            jax.ShapeDtypeStruct((T, LANES), F32),
        ],
        compiler_params=_params(("parallel",)),
        name="inproj_ab",
    )(x2d, w_main, w_lr)


def _inproj_c_kernel(x_ref, w_ref, lbnd_ref, q_ref, i_ref, gate_ref, lg_ref, *, layer_idx):
    xb = x_ref[...].astype(BF16)
    lbnd = lbnd_ref[...]
    e = jnp.exp(lbnd - jnp.max(lbnd, axis=0, keepdims=True))
    sm = e / jnp.sum(e, axis=0, keepdims=True)
    lb = jnp.sum(sm[1:layer_idx + 1], axis=0, keepdims=True)
    step = 4 * LANES
    for c in range(0, C_WIDTH, step):
        q = _dot(xb, w_ref[:, c:c + step])
        _store_groups(_silu(q) * (C_HEAD_DIM ** -0.5), q_ref, c // LANES)
    for d in range(2):
        base = (1 + d) * C_WIDTH
        for c in range(0, C_WIDTH, step):
            z = _dot(xb, w_ref[:, base + c:base + c + step])
            lbc = lb[:, c:c + step]
            f = lbc + (1.0 - lbc) * _sigmoid(z)
            _store_groups(jnp.log(f), lg_ref, (d * C_WIDTH + c) // LANES)
    for c in range(0, C_WIDTH, step):
        _store_groups(_dot(xb, w_ref[:, 3 * C_WIDTH + c:3 * C_WIDTH + c + step]), i_ref, c // LANES)
    for c in range(0, C_WIDTH, step):
        _store_groups(_dot(xb, w_ref[:, 4 * C_WIDTH + c:4 * C_WIDTH + c + step]), gate_ref, c // LANES)


def _inproj_c(x2d, w, lower_bounds, layer_idx):
    T = x2d.shape[0]
    tm = PROJ_ROWS // 2
    grp = lambda n: pl.BlockSpec((n, tm, LANES), lambda i: (0, i, 0))
    return pl.pallas_call(
        functools.partial(_inproj_c_kernel, layer_idx=layer_idx),
        grid=(T // tm,),
        in_specs=[
            pl.BlockSpec((tm, D_MODEL), lambda i: (i, 0)),
            pl.BlockSpec((D_MODEL, 5 * C_WIDTH), lambda i: (0, 0)),
            pl.BlockSpec((DEPTH, C_WIDTH), lambda i: (0, 0)),
        ],
        out_specs=[grp(C_HEADS), grp(C_HEADS), grp(C_HEADS), grp(2 * C_HEADS)],
        out_shape=[
            jax.ShapeDtypeStruct((C_HEADS, T, LANES), BF16),
            jax.ShapeDtypeStruct((C_HEADS, T, LANES), BF16),
            jax.ShapeDtypeStruct((C_HEADS, T, LANES), BF16),
            jax.ShapeDtypeStruct((2 * C_HEADS, T, LANES), F32),
        ],
        compiler_params=_params(("parallel",)),
        name="inproj_c",
    )(x2d, w, lower_bounds)


def _t5_bucket(rel):
    half = REL_BUCKETS // 2
    max_exact = half // 2
    n = np.abs(rel)
    large = max_exact + (np.log(np.maximum(n, 1) / max_exact)
                         / np.log(REL_MAX_DISTANCE / max_exact) * (half - max_exact)).astype(np.int32)
    large = np.minimum(large, half - 1)
    return np.where(rel > 0, half, 0) + np.where(n < max_exact, n, large)


def _bias_table(rel_bias, dilation):
    qi = np.arange(ATT_Q)[:, None]
    kj = np.arange(ATT_K)[None, :]
    off = kj - A_HALF_STEPS - qi
    band = np.abs(off) <= A_HALF_STEPS
    tbl = rel_bias.astype(F32)[_t5_bucket(off * dilation)]
    tbl = jnp.where(band[:, :, None], tbl, NEG)
    tbl = jnp.transpose(tbl, (2, 0, 1)).reshape(A_GROUPS, 2, ATT_Q, ATT_K)
    return jnp.transpose(tbl, (0, 2, 1, 3)).reshape(A_GROUPS, ATT_Q, 2 * ATT_K)


def _attn_kernel(q_ref, kp_ref, kc_ref, kn_ref, vp_ref, vc_ref, vn_ref, bias_ref,
                 o_ref, lse_ref, kext, vext, *, dilation, rows, seq_rows):
    j = pl.program_id(1)
    halo = A_HALF_STEPS
    kext[:, 0:halo, :] = kp_ref[...]
    kext[:, halo:halo + rows, :] = kc_ref[...]
    kext[:, halo + rows:, :] = kn_ref[...]
    vext[:, 0:halo, :] = vp_ref[...]
    vext[:, halo:halo + rows, :] = vc_ref[...]
    vext[:, halo + rows:, :] = vn_ref[...]

    low_kv = lax.broadcasted_iota(jnp.int32, (ATT_K, LANES), 1) < A_HEAD_DIM
    low_q = lax.broadcasted_iota(jnp.int32, (ATT_Q, LANES), 1) < A_HEAD_DIM
    key_in_blk = lax.broadcasted_iota(jnp.int32, (1, 2 * ATT_K), 1) & (ATT_K - 1)

    def group_body(g, carry):
        bias = bias_ref[g]
        for r in range(dilation):
            cols = slice(r * LANES, (r + 1) * LANES)
            for i in range(rows // ATT_Q):
                qrows = slice(i * ATT_Q, (i + 1) * ATT_Q)
                krows = slice(i * ATT_Q, i * ATT_Q + ATT_K)
                q2 = q_ref[g, qrows, cols] * jnp.asarray(A_HEAD_DIM ** -0.5, BF16)
                k2 = kext[g, krows, cols]
                v2 = vext[g, krows, cols]
                zero = jnp.zeros_like(k2)
                kpad = jnp.concatenate([jnp.where(low_kv, k2, zero), jnp.where(low_kv, zero, k2)], axis=0)
                vpad = jnp.concatenate([jnp.where(low_kv, v2, zero), jnp.where(low_kv, zero, v2)], axis=0)
                key_row = key_in_blk + (j * rows + (i * ATT_Q - halo))
                edge = jnp.where((key_row >= 0) & (key_row < seq_rows), 0.0, NEG)
                s = _dot_nt(q2, kpad) + bias + edge
                s0, s1 = s[:, :ATT_K], s[:, ATT_K:]
                m0 = jnp.max(s0, axis=1, keepdims=True)
                m1 = jnp.max(s1, axis=1, keepdims=True)
                p0 = jnp.exp(s0 - m0)
                p1 = jnp.exp(s1 - m1)
                l0 = jnp.sum(p0, axis=1, keepdims=True)
                l1 = jnp.sum(p1, axis=1, keepdims=True)
                p = jnp.concatenate([p0, p1], axis=1).astype(BF16)
                pv = _dot(p, vpad)
                o_ref[g, qrows, cols] = (pv * jnp.where(low_q, 1.0 / l0, 1.0 / l1)).astype(o_ref.dtype)
                lse_ref[g, qrows, cols] = jnp.where(low_q, m0 + jnp.log(l0), m1 + jnp.log(l1))
        return carry

    lax.fori_loop(0, A_GROUPS, group_body, 0)


def _attention_pattern(aqkv, bias_tbl, batch, seq, dilation):
    T = aqkv.shape[1]
    L = seq // dilation
    rows = ATT_TOKENS // dilation
    width = dilation * LANES
    halo = A_HALF_STEPS
    hb = rows // halo
    view = aqkv.reshape(3 * A_GROUPS, batch, L, width)
    cur = lambda part: pl.BlockSpec((A_GROUPS, None, rows, width), lambda b, j: (part, b, j, 0))
    prev = lambda part: pl.BlockSpec((A_GROUPS, None, halo, width),
                                     lambda b, j: (part, b, jnp.maximum(j * hb - 1, 0), 0))
    nxt = lambda part: pl.BlockSpec((A_GROUPS, None, halo, width),
                                    lambda b, j: (part, b, jnp.minimum((j + 1) * hb, L // halo - 1), 0))
    out_spec = pl.BlockSpec((A_GROUPS, None, rows, width), lambda b, j: (0, b, j, 0))
    o, lse = pl.pallas_call(
        functools.partial(_attn_kernel, dilation=dilation, rows=rows, seq_rows=L),
        grid=(batch, L // rows),
        in_specs=[cur(0), prev(1), cur(1), nxt(1), prev(2), cur(2), nxt(2),
                  pl.BlockSpec((A_GROUPS, ATT_Q, 2 * ATT_K), lambda b, j: (0, 0, 0))],
        out_specs=[out_spec, out_spec],
        out_shape=[jax.ShapeDtypeStruct((A_GROUPS, batch, L, width), BF16),
                   jax.ShapeDtypeStruct((A_GROUPS, batch, L, width), F32)],
        scratch_shapes=[pltpu.VMEM((A_GROUPS, rows + 2 * halo, width), BF16),
                        pltpu.VMEM((A_GROUPS, rows + 2 * halo, width), BF16)],
        compiler_params=_params(("parallel", "parallel")),
        name=f"dilated_attn_d{dilation}",
    )(view, view, view, view, view, view, view, bias_tbl)
    return o.reshape(A_GROUPS, T, LANES), lse.reshape(A_GROUPS, T, LANES)


def _scan_constants():
    t = np.arange(ROW_BLOCK)[:, None]
    r = np.arange(ROW_BLOCK)[None, :]
    same = t // CHUNK == r // CHUNK
    totals = np.repeat(np.arange(ROW_BLOCK // CHUNK), TOTAL_ROWS)[:, None] == r // CHUNK
    prefix = same & (r <= t)
    suffix = same & (r >= t)
    state_f = np.concatenate([same & (r > t), totals], axis=0)
    state_b = np.concatenate([same & (r < t), totals], axis=0)
    as_bf16 = lambda m: jnp.asarray(m.astype(np.float32), BF16)
    return as_bf16(prefix), as_bf16(suffix), as_bf16(state_f), as_bf16(state_b)


def _intra_masks():
    t = lax.broadcasted_iota(jnp.int32, (CHUNK, CHUNK), 0)
    s = lax.broadcasted_iota(jnp.int32, (CHUNK, CHUNK), 1)
    tb, sb = t // SUB, s // SUB
    return (tb == sb) & (s <= t), tb > sb, (tb == sb) & (s > t), tb < sb


N_OPERANDS = 5


def _decayed_operands(cum_ref, q_rows, k_rows, op_ref, backward):
    per_chunk = CHUNK // SUB
    half = SUB // 2
    for b in range(ROW_BLOCK // SUB):
        r0 = b * SUB
        rows = slice(r0, r0 + SUB)
        row = lambda i: cum_ref[i:i + 1, :]
        if backward:
            mid, r_out = row(r0 + half), row(r0)
            r_in = None if b % per_chunk == per_chunk - 1 else row(r0 + SUB)
        else:
            mid, r_out = row(r0 + half - 1), row(r0 + SUB - 1)
            r_in = None if b % per_chunk == 0 else row(r0 - 1)
        p = cum_ref[rows, :]
        q, k = q_rows(rows), k_rows(rows)
        d_mid = p - mid
        q_in = q * jnp.exp(p if r_in is None else p - r_in)
        q_dec = q_in if r_in is None else q_in * jnp.exp(r_in)
        for i, val in enumerate((q * jnp.exp(d_mid), k * jnp.exp(-d_mid), q_in, k * jnp.exp(r_out - p), q_dec)):
            op_ref[i, rows, :] = val.astype(BF16)


def _chunk_output(ci, opf, opb, v, stf, stb, masks):
    m_df, m_of, m_db, m_ob = masks
    rows = slice(ci * CHUNK, (ci + 1) * CHUNK)
    a = jnp.where(m_df, _dot_nt(opf[0, rows, :], opf[1, rows, :]), 0.0)
    a = a + jnp.where(m_of, _dot_nt(opf[2, rows, :], opf[3, rows, :]), 0.0)
    a = a + jnp.where(m_db, _dot_nt(opb[0, rows, :], opb[1, rows, :]), 0.0)
    a = a + jnp.where(m_ob, _dot_nt(opb[2, rows, :], opb[3, rows, :]), 0.0)
    return _dot(a.astype(BF16), v) + _dot_nt(opf[4, rows, :], stf) + _dot_nt(opb[4, rows, :], stb)


def _state_pass(n_heads, nv, get_lg, get_k, v_ref, msel, out_ref, st_ref, kdec_scr, etot_scr, reverse):
    n_chunks = ROW_BLOCK // CHUNK
    order = range(n_chunks - 1, -1, -1) if reverse else range(n_chunks)
    for h in range(n_heads):
        lg = get_lg(h)
        d = _dot_sel(msel, lg)
        kdec_scr[h] = (get_k(h, lg) * jnp.exp(d[0:ROW_BLOCK])).astype(BF16)
        etot_scr[h] = jnp.exp(d[ROW_BLOCK:])
    for h in range(n_heads):
        ds = {}
        for ci in order:
            rows = slice(ci * CHUNK, (ci + 1) * CHUNK)
            ds[ci] = _dot_tn(_head_value(v_ref, h, rows, nv), kdec_scr[h, rows, :])
        st = st_ref[h]
        for ci in order:
            out_ref[ci, h] = st.astype(out_ref.dtype)
            st = st * etot_scr[h, ci * TOTAL_ROWS:ci * TOTAL_ROWS + 1, :] + ds[ci]
        st_ref[h] = st


def _head_value(v_ref, h, rows, nv):
    if nv == 1:
        return v_ref[h, rows, :]
    return jnp.concatenate([v_ref[nv * h + k, rows, :] for k in range(nv)], axis=1)


def _gla_log_gates(lr, up_pad, bias_pad, lg_scr):
    z = _dot_f32(lr, up_pad) + bias_pad
    lg = _log_sigmoid(z) * (1.0 / B_GATE_NORMALIZER)
    for g in range(2 * B_HEADS):
        lg_scr[g] = lg[:, g * LANES:(g + 1) * LANES]


def _state_kernel_c(lg_ref, v_ref, msel_ref, out_ref, st_ref, kdec_scr, etot_scr, *, reverse):
    @pl.when(pl.program_id(1) == 0)
    def _():
        st_ref[...] = jnp.zeros_like(st_ref)

    _state_pass(C_HEADS, 1, lambda h: lg_ref[h], lambda h, lg: 1.0 - jnp.exp(lg), v_ref, msel_ref[...],
                out_ref, st_ref, kdec_scr, etot_scr, reverse)


def _state_kernel_b(lr_ref, k_ref, v_ref, up_ref, bias_ref, msel_ref, out_ref, st_ref, lg_scr, kdec_scr,
                    etot_scr, *, reverse):
    @pl.when(pl.program_id(1) == 0)
    def _():
        st_ref[...] = jnp.zeros_like(st_ref)

    _gla_log_gates(lr_ref[...], up_ref[...], bias_ref[...], lg_scr)
    gate0 = B_HEADS if reverse else 0
    _state_pass(B_HEADS, B_VAL_DIM // LANES, lambda h: lg_scr[gate0 + h], lambda h, lg: k_ref[h].astype(F32),
                v_ref, msel_ref[...], out_ref, st_ref, kdec_scr, etot_scr, reverse)


def _seq_block_index(reverse, blocks_per_seq):
    if reverse:
        return lambda b, n: b * blocks_per_seq + (blocks_per_seq - 1 - n)
    return lambda b, n: b * blocks_per_seq + n


def _states_c(lg, v, msel, batch, seq, reverse):
    T = v.shape[1]
    nb = seq // ROW_BLOCK
    cpb = ROW_BLOCK // CHUNK
    blk = _seq_block_index(reverse, nb)
    gate_part = 1 if reverse else 0
    return pl.pallas_call(
        functools.partial(_state_kernel_c, reverse=reverse),
        grid=(batch, nb),
        in_specs=[
            pl.BlockSpec((C_HEADS, ROW_BLOCK, LANES), lambda b, n: (gate_part, blk(b, n), 0)),
            pl.BlockSpec((C_HEADS, ROW_BLOCK, LANES), lambda b, n: (0, blk(b, n), 0)),
            pl.BlockSpec(msel.shape, lambda b, n: (0, 0)),
        ],
        out_specs=pl.BlockSpec((cpb, C_HEADS, C_HEAD_DIM, C_HEAD_DIM), lambda b, n: (blk(b, n), 0, 0, 0)),
        out_shape=jax.ShapeDtypeStruct((T // CHUNK, C_HEADS, C_HEAD_DIM, C_HEAD_DIM), BF16),
        scratch_shapes=[pltpu.VMEM((C_HEADS, C_HEAD_DIM, C_HEAD_DIM), F32),
                        pltpu.VMEM((C_HEADS, ROW_BLOCK, LANES), BF16),
                        pltpu.VMEM((C_HEADS, cpb * TOTAL_ROWS, LANES), F32)],
        compiler_params=_params(("parallel", "arbitrary")),
        name="hgrn_states_bwd" if reverse else "hgrn_states_fwd",
    )(lg, v, msel)


def _states_b(lr, bqk, bv, up_pad, bias_pad, msel, batch, seq, reverse):
    T = lr.shape[0]
    nb = seq // ROW_BLOCK
    cpb = ROW_BLOCK // CHUNK
    blk = _seq_block_index(reverse, nb)
    full = lambda a: pl.BlockSpec(a.shape, lambda b, n: (0,) * a.ndim)
    return pl.pallas_call(
        functools.partial(_state_kernel_b, reverse=reverse),
        grid=(batch, nb),
        in_specs=[
            pl.BlockSpec((ROW_BLOCK, LANES), lambda b, n: (blk(b, n), 0)),
            pl.BlockSpec((B_HEADS, ROW_BLOCK, LANES), lambda b, n: (1, blk(b, n), 0)),
            pl.BlockSpec((2 * B_HEADS, ROW_BLOCK, LANES), lambda b, n: (0, blk(b, n), 0)),
            full(up_pad), full(bias_pad), full(msel),
        ],
        out_specs=pl.BlockSpec((cpb, B_HEADS, B_VAL_DIM, B_KEY_DIM), lambda b, n: (blk(b, n), 0, 0, 0)),
        out_shape=jax.ShapeDtypeStruct((T // CHUNK, B_HEADS, B_VAL_DIM, B_KEY_DIM), BF16),
        scratch_shapes=[pltpu.VMEM((B_HEADS, B_VAL_DIM, B_KEY_DIM), F32),
                        pltpu.VMEM((2 * B_HEADS, ROW_BLOCK, LANES), F32),
                        pltpu.VMEM((B_HEADS, ROW_BLOCK, LANES), BF16),
                        pltpu.VMEM((B_HEADS, cpb * TOTAL_ROWS, LANES), F32)],
        compiler_params=_params(("parallel", "arbitrary")),
        name="gla_states_bwd" if reverse else "gla_states_fwd",
    )(lr, bqk, bv, up_pad, bias_pad, msel)


def _project_and_norm(y_scr, wout_ref, x_ref, lng_ref, lnb_ref, out_ref):
    y = jnp.concatenate([y_scr[g] for g in range(MIX_GROUPS)], axis=1)
    z = DEEPNORM_ALPHA * x_ref[...] + _dot(y, wout_ref[...])
    out_ref[...] = _layer_norm(z, lng_ref[...], lnb_ref[...])


def _rms_gain(o, gain):
    return o * lax.rsqrt(jnp.mean(o * o, axis=-1, keepdims=True) + NORM_EPS) * gain


def _scan_heads(n_heads, nv, lg_of, q_rows_of, k_rows_of, v_ref, sf_ref, sb_ref, mpre, msuf, cum_scr, op_scr,
                finish):
    masks = _intra_masks()

    def step(i, carry):
        heads = [i * HEADS_PER_STEP + j for j in range(HEADS_PER_STEP)]
        for j, h in enumerate(heads):
            cum_scr[j, 0] = _dot_sel(mpre, lg_of(h, 0))
            cum_scr[j, 1] = _dot_sel(msuf, lg_of(h, 1))
        for j, h in enumerate(heads):
            for d in range(2):
                _decayed_operands(cum_scr.at[j, d], q_rows_of(h), k_rows_of(h, d), op_scr.at[j, d], d == 1)
        for j, h in enumerate(heads):
            for ci in range(ROW_BLOCK // CHUNK):
                rows = slice(ci * CHUNK, (ci + 1) * CHUNK)
                o = _chunk_output(ci, op_scr.at[j, 0], op_scr.at[j, 1], _head_value(v_ref, h, rows, nv),
                                  sf_ref[ci, h], sb_ref[ci, h], masks)
                finish(h, rows, o)
        return carry

    lax.fori_loop(0, n_heads // HEADS_PER_STEP, step, 0)


def _out_kernel_c(q_ref, i_ref, gate_ref, lgf_ref, lgb_ref, sf_ref, sb_ref, mpre_ref, msuf_ref, gain_ref,
                  wout_ref, x_ref, lng_ref, lnb_ref, out_ref, y_scr, cum_scr, op_scr):
    lg_refs = (lgf_ref, lgb_ref)

    def finish(h, rows, o):
        y = _rms_gain(o, gain_ref[h]) * _silu(gate_ref[h, rows, :].astype(F32))
        y_scr[h, rows, :] = y.astype(BF16)

    _scan_heads(C_HEADS, 1,
                lambda h, d: lg_refs[d][h],
                lambda h: lambda rows: q_ref[h, rows, :].astype(F32),
                lambda h, d: lambda rows: 1.0 - jnp.exp(lg_refs[d][h, rows, :]),
                i_ref, sf_ref, sb_ref, mpre_ref[...], msuf_ref[...], cum_scr, op_scr, finish)
    _project_and_norm(y_scr, wout_ref, x_ref, lng_ref, lnb_ref, out_ref)


def _out_kernel_ab(o1_ref, o2_ref, o3_ref, l1_ref, l2_ref, l3_ref, qk_ref, v_ref, gate_ref, lr_ref,
                   sf_ref, sb_ref, mf_ref, mb_ref, up_ref, gbias_ref, gain_ref,
                   wout_ref, x_ref, lng_ref, lnb_ref, out_ref, y_scr, lg_scr, cum_scr, op_scr):
    for g in range(A_GROUPS):
        l1, l2, l3 = l1_ref[g], l2_ref[g], l3_ref[g]
        m = jnp.maximum(jnp.maximum(l1, l2), l3)
        e1, e2, e3 = jnp.exp(l1 - m), jnp.exp(l2 - m), jnp.exp(l3 - m)
        num = e1 * o1_ref[g].astype(F32) + e2 * o2_ref[g].astype(F32) + e3 * o3_ref[g].astype(F32)
        oa = num / (e1 + e2 + e3)
        y_scr[g] = (oa * _silu(gate_ref[g].astype(F32))).astype(BF16)

    _gla_log_gates(lr_ref[...], up_ref[...], gbias_ref[...], lg_scr)
    nv = B_VAL_DIM // LANES

    def finish(h, rows, o):
        gain = jnp.concatenate([gain_ref[nv * h + k] for k in range(nv)], axis=1)
        o = _rms_gain(o, gain)
        for kk in range(nv):
            grp = A_GROUPS + nv * h + kk
            y = o[:, kk * LANES:(kk + 1) * LANES] * _silu(gate_ref[grp, rows, :].astype(F32))
            y_scr[grp, rows, :] = y.astype(BF16)

    _scan_heads(B_HEADS, nv,
                lambda h, d: lg_scr[d * B_HEADS + h],
                lambda h: lambda rows: qk_ref[h, rows, :].astype(F32),
                lambda h, d: lambda rows: qk_ref[B_HEADS + h, rows, :].astype(F32),
                v_ref, sf_ref, sb_ref, mf_ref[...], mb_ref[...], cum_scr, op_scr, finish)
    _project_and_norm(y_scr, wout_ref, x_ref, lng_ref, lnb_ref, out_ref)


def _scan_scratch(tb):
    return [pltpu.VMEM((HEADS_PER_STEP, 2, tb, LANES), F32),
            pltpu.VMEM((HEADS_PER_STEP, 2, N_OPERANDS, tb, LANES), BF16)]


def _row_specs(tb):
    grp = lambda n, part=0: pl.BlockSpec((n, tb, LANES), lambda i: (part, i, 0))
    full = lambda a: pl.BlockSpec(a.shape, lambda i: (0,) * a.ndim)
    rows = lambda width: pl.BlockSpec((tb, width), lambda i: (i, 0))
    return grp, full, rows


def _output_c(cq, ci_, gate, lg, sf, sb, mf, mb, gain, wout, x2d, lng, lnb):
    T = x2d.shape[0]
    tb = ROW_BLOCK
    cpb = tb // CHUNK
    grp, full, rows = _row_specs(tb)
    st = pl.BlockSpec((cpb, C_HEADS, C_HEAD_DIM, C_HEAD_DIM), lambda i: (i, 0, 0, 0))
    return pl.pallas_call(
        _out_kernel_c,
        grid=(T // tb,),
        in_specs=[grp(C_HEADS), grp(C_HEADS), grp(C_HEADS), grp(C_HEADS, 0), grp(C_HEADS, 1), st, st,
                  full(mf), full(mb), full(gain), full(wout), rows(D_MODEL), full(lng), full(lnb)],
        out_specs=rows(D_MODEL),
        out_shape=jax.ShapeDtypeStruct((T, D_MODEL), F32),
        scratch_shapes=[pltpu.VMEM((MIX_GROUPS, tb, LANES), BF16)] + _scan_scratch(tb),
        compiler_params=_params(("parallel",)),
        name="hgrn_output",
    )(cq, ci_, gate, lg, lg, sf, sb, mf, mb, gain, wout, x2d, lng, lnb)


def _output_ab(attn, bqk, bv, gate, lr, sf, sb, mf, mb, up_pad, gbias_pad, gain, wout, x2d, lng, lnb):
    T = x2d.shape[0]
    tb = ROW_BLOCK
    cpb = tb // CHUNK
    grp, full, rows = _row_specs(tb)
    st = pl.BlockSpec((cpb, B_HEADS, B_VAL_DIM, B_KEY_DIM), lambda i: (i, 0, 0, 0))
    (o1, l1), (o2, l2), (o3, l3) = attn
    return pl.pallas_call(
        _out_kernel_ab,
        grid=(T // tb,),
        in_specs=[grp(A_GROUPS)] * 6 + [grp(2 * B_HEADS), grp(8), grp(MIX_GROUPS), rows(LANES), st, st,
                                        full(mf), full(mb), full(up_pad), full(gbias_pad), full(gain),
                                        full(wout), rows(D_MODEL), full(lng), full(lnb)],
        out_specs=rows(D_MODEL),
        out_shape=jax.ShapeDtypeStruct((T, D_MODEL), F32),
        scratch_shapes=[pltpu.VMEM((MIX_GROUPS, tb, LANES), BF16),
                        pltpu.VMEM((2 * B_HEADS, tb, LANES), F32)] + _scan_scratch(tb),
        compiler_params=_params(("parallel",)),
        name="ab_output",
    )(o1, o2, o3, l1, l2, l3, bqk, bv, gate, lr, sf, sb, mf, mb, up_pad, gbias_pad, gain, wout, x2d, lng, lnb)


def _layer_ab(x2d, batch, seq, w_in, gate_up, gate_bias, norm_gain, w_out, rel_bias, lng, lnb, consts):
    mf, mb, msel_f, msel_b = consts
    sizes = (3 * A_WIDTH + 2 * B_K_WIDTH + B_V_WIDTH, 2 * B_GATE_RANK, MIX_WIDTH)
    o0, o1 = sizes[0], sizes[0] + sizes[1]
    w_main = jnp.concatenate([w_in[:, :o0], w_in[:, o1:]], axis=1).astype(BF16)
    w_lr = jnp.pad(w_in[:, o0:o1], ((0, 0), (0, LANES - sizes[1]))).astype(BF16)
    up = gate_up.astype(F32)
    up_pad = jnp.zeros((LANES, 2 * B_K_WIDTH), F32)
    up_pad = up_pad.at[0:B_GATE_RANK, 0:B_K_WIDTH].set(up[0])
    up_pad = up_pad.at[B_GATE_RANK:2 * B_GATE_RANK, B_K_WIDTH:].set(up[1])
    gbias_pad = gate_bias.astype(F32).reshape(1, 2 * B_K_WIDTH)

    aqkv, bqk, bv, gate, lr = _inproj_ab(x2d, w_main, w_lr)
    attn = [_attention_pattern(aqkv, _bias_table(rel_bias, d), batch, seq, d) for d in DILATIONS]
    sf = _states_b(lr, bqk, bv, up_pad, gbias_pad, msel_f, batch, seq, reverse=False)
    sb = _states_b(lr, bqk, bv, up_pad, gbias_pad, msel_b, batch, seq, reverse=True)
    gain = norm_gain.astype(F32).reshape(B_V_WIDTH // LANES, 1, LANES)
    wout = w_out.astype(BF16)
    return _output_ab(attn, bqk, bv, gate, lr, sf, sb, mf, mb, up_pad, gbias_pad, gain, wout, x2d, lng, lnb)


def _layer_c(x2d, batch, seq, w_in, lower_bounds, layer_idx, norm_gain, w_out, lng, lnb, consts):
    mf, mb, msel_f, msel_b = consts
    cq, ci_, gate, lg = _inproj_c(x2d, w_in.astype(BF16), lower_bounds.astype(F32), layer_idx)
    sf = _states_c(lg, ci_, msel_f, batch, seq, reverse=False)
    sb = _states_c(lg, ci_, msel_b, batch, seq, reverse=True)
    gain = norm_gain.astype(F32).reshape(C_HEADS, 1, LANES)
    wout = w_out.astype(BF16)
    return _output_c(cq, ci_, gate, lg, sf, sb, mf, mb, gain, wout, x2d, lng, lnb)


def kernel(x, w_in_ab, gla_gate_up, gla_gate_bias, gla_norm, w_out_ab, w_in_c, hgrn_lower_bounds, hgrn_norm,
           w_out_c, rel_bias, ln_gain, ln_bias):
    batch, seq, _ = x.shape
    assert seq % ATT_TOKENS == 0 and seq % ROW_BLOCK == 0 and (batch * seq) % PROJ_ROWS == 0
    consts = _scan_constants()
    x2d = x.astype(F32).reshape(batch * seq, D_MODEL)
    for layer in range(DEPTH):
        lng = ln_gain[layer].astype(F32).reshape(1, D_MODEL)
        lnb = ln_bias[layer].astype(F32).reshape(1, D_MODEL)
        if layer % 2 == 0:
            e = layer // 2
            x2d = _layer_ab(x2d, batch, seq, w_in_ab[e], gla_gate_up[e], gla_gate_bias[e], gla_norm[e],
                            w_out_ab[e], rel_bias, lng, lnb, consts)
        else:
            o = layer // 2
            x2d = _layer_c(x2d, batch, seq, w_in_c[o], hgrn_lower_bounds, layer, hgrn_norm[o], w_out_c[o],
                           lng, lnb, consts)
    return x2d.reshape(batch, seq, D_MODEL).astype(x.dtype)
```

```python
import functools

import numpy as np
import jax
import jax.numpy as jnp
from jax import lax
from jax.experimental import pallas as pl
from jax.experimental.pallas import tpu as pltpu

F32 = jnp.float32
BF16 = jnp.bfloat16

LANES = 128
VMEM_LIMIT = 56 * 1024 * 1024

D_MODEL = 1024
DEPTH = 4
A_HEADS = 8
A_HEAD_DIM = 64
A_WIDTH = A_HEADS * A_HEAD_DIM
A_GROUPS = A_WIDTH // LANES
DILATIONS = (1, 4, 16)
A_HALF_STEPS = 64
REL_BUCKETS = 32
REL_MAX_DISTANCE = 1024
B_HEADS = 4
B_KEY_DIM = 128
B_VAL_DIM = 256
B_K_WIDTH = B_HEADS * B_KEY_DIM
B_V_WIDTH = B_HEADS * B_VAL_DIM
B_GATE_RANK = 16
B_GATE_NORMALIZER = 16.0
C_HEADS = 12
C_HEAD_DIM = 128
C_WIDTH = C_HEADS * C_HEAD_DIM
MIX_WIDTH = A_WIDTH + B_V_WIDTH
MIX_GROUPS = MIX_WIDTH // LANES
NORM_EPS = 1e-5
DEEPNORM_ALPHA = (2 * DEPTH) ** 0.25

NEG = -1e30
LOG2E = 1.4426950408889634
LN2 = 0.6931471805599453
ATT_Q_SCALE = A_HEAD_DIM ** -0.5 * LOG2E

CHUNK = 64
SUB = 32
ATT_Q = 128
ATT_K = ATT_Q + 2 * A_HALF_STEPS
ATT_TOKENS = 2048
ATT_BATCH = 8
ROW_BLOCK = 256
TOTAL_ROWS = 8
HEADS_PER_STEP = 4
PROJ_ROWS = 512


def _params(sem):
    return pltpu.CompilerParams(dimension_semantics=sem, vmem_limit_bytes=VMEM_LIMIT)


def _dot(a, b):
    return jnp.dot(a, b, preferred_element_type=F32)


def _dot_nt(a, b):
    return lax.dot_general(a, b, (((1,), (1,)), ((), ())), preferred_element_type=F32)


def _dot_tn(a, b):
    return lax.dot_general(a, b, (((0,), (0,)), ((), ())), preferred_element_type=F32)


def _split2(x):
    hi = x.astype(BF16)
    lo = (x - hi.astype(F32)).astype(BF16)
    return hi, lo


def _dot_sel(m01, x):
    hi, lo = _split2(x)
    return _dot(m01, hi) + _dot(m01, lo)


def _dot_f32(a, b):
    ah, al = _split2(a)
    bh, bl = _split2(b)
    return _dot(ah, bh) + _dot(ah, bl) + _dot(al, bh)


def _sigmoid(z):
    return 1.0 / (1.0 + jnp.exp(-z))


def _silu(z):
    return z * _sigmoid(z)


def _log_sigmoid(z):
    return jnp.minimum(z, 0.0) - jnp.log(1.0 + jnp.exp(-jnp.abs(z)))


def _layer_norm(z, g, b):
    mu = jnp.mean(z, axis=-1, keepdims=True)
    zc = z - mu
    var = jnp.mean(zc * zc, axis=-1, keepdims=True)
    return zc * lax.rsqrt(var + NORM_EPS) * g + b


def _store_groups(acc, out_ref, g0):
    for k in range(acc.shape[1] // LANES):
        out_ref[g0 + k] = acc[:, k * LANES:(k + 1) * LANES].astype(out_ref.dtype)


def _inproj_ab_kernel(x_ref, w_ref, wlr_ref, a1_ref, a4_ref, a16_ref, bqk_ref, bv_ref, gate_ref, lr_ref, nat_scr):
    xb = x_ref[...].astype(BF16)
    step = 4 * LANES
    tm = x_ref.shape[0]
    for c in range(0, 3 * A_WIDTH, step):
        acc = _dot(xb, w_ref[:, c:c + step])
        if c == 0:
            acc = acc * ATT_Q_SCALE
        g0 = c // LANES
        for k in range(A_GROUPS):
            nat_scr[k] = acc[:, k * LANES:(k + 1) * LANES]
        for d, ref in zip(DILATIONS, (a1_ref, a4_ref, a16_ref)):
            for k in range(A_GROUPS):
                for r in range(d):
                    src = nat_scr[k] if d == 1 else nat_scr[k, pl.ds(r, tm // d, stride=d), :]
                    ref[g0 + k, r] = src.astype(BF16)
    base = 3 * A_WIDTH
    for c in range(0, 2 * B_K_WIDTH, step):
        acc = _dot(xb, w_ref[:, base + c:base + c + step])
        if c < B_K_WIDTH:
            acc = acc * (B_KEY_DIM ** -0.5)
        _store_groups(acc, bqk_ref, c // LANES)
    base += 2 * B_K_WIDTH
    for c in range(0, B_V_WIDTH, step):
        _store_groups(_dot(xb, w_ref[:, base + c:base + c + step]), bv_ref, c // LANES)
    base += B_V_WIDTH
    for c in range(0, MIX_WIDTH, step):
        _store_groups(_dot(xb, w_ref[:, base + c:base + c + step]), gate_ref, c // LANES)
    lr_ref[...] = _dot(xb, wlr_ref[...])


def _inproj_ab(x2d, w_main, w_lr, batch, seq):
    T = x2d.shape[0]
    tm = PROJ_ROWS
    n_main = w_main.shape[1]
    nbs = seq // tm
    grp = lambda n: pl.BlockSpec((n, tm, LANES), lambda i: (0, i, 0))
    split = lambda d: pl.BlockSpec((3 * A_GROUPS, None, d, tm // d, LANES),
                                   lambda i: (0, i // nbs, 0, i % nbs, 0))
    split_shape = lambda d: jax.ShapeDtypeStruct((3 * A_GROUPS, batch, d, seq // d, LANES), BF16)
    return pl.pallas_call(
        _inproj_ab_kernel,
        grid=(T // tm,),
        in_specs=[
            pl.BlockSpec((tm, D_MODEL), lambda i: (i, 0)),
            pl.BlockSpec((D_MODEL, n_main), lambda i: (0, 0)),
            pl.BlockSpec((D_MODEL, LANES), lambda i: (0, 0)),
        ],
        out_specs=[split(d) for d in DILATIONS] + [grp(8), grp(8), grp(MIX_GROUPS),
                                                   pl.BlockSpec((tm, LANES), lambda i: (i, 0))],
        scratch_shapes=[pltpu.VMEM((A_GROUPS, tm, LANES), F32)],
        out_shape=[split_shape(d) for d in DILATIONS] + [
            jax.ShapeDtypeStruct((8, T, LANES), BF16),
            jax.ShapeDtypeStruct((8, T, LANES), BF16),
            jax.ShapeDtypeStruct((MIX_GROUPS, T, LANES), BF16),
            jax.ShapeDtypeStruct((T, LANES), F32),
        ],
        compiler_params=_params(("parallel",)),
        name="inproj_ab",
    )(x2d, w_main, w_lr)


def _inproj_c_kernel(x_ref, w_ref, lbnd_ref, q_ref, i_ref, gate_ref, lg_ref, *, layer_idx):
    xb = x_ref[...].astype(BF16)
    lbnd = lbnd_ref[...]
    e = jnp.exp(lbnd - jnp.max(lbnd, axis=0, keepdims=True))
    sm = e / jnp.sum(e, axis=0, keepdims=True)
    lb = jnp.sum(sm[1:layer_idx + 1], axis=0, keepdims=True)
    step = 4 * LANES
    for c in range(0, C_WIDTH, step):
        q = _dot(xb, w_ref[:, c:c + step])
        _store_groups(_silu(q) * (C_HEAD_DIM ** -0.5), q_ref, c // LANES)
    for d in range(2):
        base = (1 + d) * C_WIDTH
        for c in range(0, C_WIDTH, step):
            z = _dot(xb, w_ref[:, base + c:base + c + step])
            lbc = lb[:, c:c + step]
            f = lbc + (1.0 - lbc) * _sigmoid(z)
            _store_groups(jnp.log2(f), lg_ref, (d * C_WIDTH + c) // LANES)
    for c in range(0, C_WIDTH, step):
        _store_groups(_dot(xb, w_ref[:, 3 * C_WIDTH + c:3 * C_WIDTH + c + step]), i_ref, c // LANES)
    for c in range(0, C_WIDTH, step):
        _store_groups(_dot(xb, w_ref[:, 4 * C_WIDTH + c:4 * C_WIDTH + c + step]), gate_ref, c // LANES)


def _inproj_c(x2d, w, lower_bounds, layer_idx):
    T = x2d.shape[0]
    tm = PROJ_ROWS // 2
    grp = lambda n: pl.BlockSpec((n, tm, LANES), lambda i: (0, i, 0))
    return pl.pallas_call(
        functools.partial(_inproj_c_kernel, layer_idx=layer_idx),
        grid=(T // tm,),
        in_specs=[
            pl.BlockSpec((tm, D_MODEL), lambda i: (i, 0)),
            pl.BlockSpec((D_MODEL, 5 * C_WIDTH), lambda i: (0, 0)),
            pl.BlockSpec((DEPTH, C_WIDTH), lambda i: (0, 0)),
        ],
        out_specs=[grp(C_HEADS), grp(C_HEADS), grp(C_HEADS), grp(2 * C_HEADS)],
        out_shape=[
            jax.ShapeDtypeStruct((C_HEADS, T, LANES), BF16),
            jax.ShapeDtypeStruct((C_HEADS, T, LANES), BF16),
            jax.ShapeDtypeStruct((C_HEADS, T, LANES), BF16),
            jax.ShapeDtypeStruct((2 * C_HEADS, T, LANES), F32),
        ],
        compiler_params=_params(("parallel",)),
        name="inproj_c",
    )(x2d, w, lower_bounds)


def _t5_bucket(rel):
    half = REL_BUCKETS // 2
    max_exact = half // 2
    n = np.abs(rel)
    large = max_exact + (np.log(np.maximum(n, 1) / max_exact)
                         / np.log(REL_MAX_DISTANCE / max_exact) * (half - max_exact)).astype(np.int32)
    large = np.minimum(large, half - 1)
    return np.where(rel > 0, half, 0) + np.where(n < max_exact, n, large)


def _bias_tables(rel_bias, dilation):
    offs = np.arange(-A_HALF_STEPS, A_HALF_STEPS + 1)
    vals = rel_bias.astype(F32)[_t5_bucket(offs * dilation)] * LOG2E
    period = ATT_K + ATT_Q
    ext = jnp.concatenate([vals, jnp.full((period - vals.shape[0], A_HEADS), NEG, F32)], axis=0).T
    flat = jnp.broadcast_to(ext[:, None, :], (A_HEADS, ATT_Q, period)).reshape(A_HEADS, ATT_Q * period)
    tbl = flat[:, :ATT_Q * (period - 1)].reshape(A_HEADS, ATT_Q, period - 1)[:, :, :ATT_K]
    col = np.arange(ATT_K)
    before, after = col < A_HALF_STEPS, col >= ATT_K - A_HALF_STEPS
    variants = []
    for mask in (np.zeros_like(before), before, after, before | after):
        t = jnp.where(mask[None, None, :], NEG, tbl).reshape(A_GROUPS, 2, ATT_Q, ATT_K)
        variants.append(jnp.transpose(t, (0, 2, 1, 3)).reshape(A_GROUPS, ATT_Q, 2 * ATT_K))
    return jnp.stack(variants, axis=0)


def _attn_kernel(q_ref, kp_ref, kc_ref, kn_ref, vp_ref, vc_ref, vn_ref, bias_ref, o_ref, lse_ref, *,
                 dilation, rows):
    j = pl.program_id(1)
    first = (j == 0).astype(jnp.int32)
    last = (j == pl.num_programs(1) - 1).astype(jnp.int32)
    halo = A_HALF_STEPS
    n_sub = rows // ATT_Q
    low_kv = lax.broadcasted_iota(jnp.int32, (ATT_K, LANES), 1) < A_HEAD_DIM
    low_q = lax.broadcasted_iota(jnp.int32, (ATT_Q, LANES), 1) < A_HEAD_DIM
    zero = jnp.zeros((ATT_K, LANES), BF16)
    one = jnp.ones((ATT_K, LANES), BF16)

    def window(prev_ref, cur_ref, next_ref, g, r, i):
        lo, hi = i * ATT_Q - halo, i * ATT_Q - halo + ATT_K
        parts = [prev_ref[g, r]] if lo < 0 else []
        parts.append(cur_ref[g, r, max(lo, 0):min(hi, rows), :])
        if hi > rows:
            parts.append(next_ref[g, r])
        return parts[0] if len(parts) == 1 else jnp.concatenate(parts, axis=0)

    units = [(r, i) for r in range(dilation) for i in range(n_sub)]

    def scores(g, r, i):
        variant = (first if i == 0 else 0) + (2 * last if i == n_sub - 1 else 0)
        bias = bias_ref[variant, g]
        q2 = q_ref[g, r, i * ATT_Q:(i + 1) * ATT_Q, :]
        k2 = window(kp_ref, kc_ref, kn_ref, g, r, i)
        s0 = _dot_nt(q2, jnp.where(low_kv, k2, zero)) + bias[:, :ATT_K]
        s1 = _dot_nt(q2, jnp.where(low_kv, zero, k2)) + bias[:, ATT_K:]
        return s0, s1

    def softmax(s):
        m = jnp.max(s, axis=1, keepdims=True)
        return m, jnp.exp2(s - m).astype(BF16)

    def finish(g, r, i, m0, p0, m1, p1):
        v2 = window(vp_ref, vc_ref, vn_ref, g, r, i)
        r0 = _dot(p0, jnp.where(low_kv, v2, one))
        r1 = _dot(p1, jnp.where(low_kv, one, v2))
        pv = jnp.where(low_q, r0, r1)
        denom = pltpu.roll(jnp.where(low_q, r1, r0), A_HEAD_DIM, axis=1)
        m = jnp.where(low_q, m0, m1)
        start = r + dilation * i * ATT_Q
        dst = pl.ds(start, ATT_Q) if dilation == 1 else pl.ds(start, ATT_Q, stride=dilation)
        o_ref[g, dst, :] = pv / denom
        lse_ref[g, dst, :] = LN2 * (m + jnp.log2(denom))

    def group_body(g, carry):
        for u0 in range(0, len(units), ATT_BATCH):
            batch = units[u0:u0 + ATT_BATCH]
            s = [scores(g, r, i) for r, i in batch]
            sm = [(softmax(s0), softmax(s1)) for s0, s1 in s]
            for (r, i), ((m0, p0), (m1, p1)) in zip(batch, sm):
                finish(g, r, i, m0, p0, m1, p1)
        return carry

    lax.fori_loop(0, A_GROUPS, group_body, 0)


def _attention_pattern(a_split, bias_tbl, dilation):
    _, batch, _, L, _ = a_split.shape
    T = batch * L * dilation
    rows = ATT_TOKENS // dilation
    halo = A_HALF_STEPS
    hb = rows // halo
    nj = L // rows
    cur = lambda part: pl.BlockSpec((A_GROUPS, None, dilation, rows, LANES), lambda b, j: (part, b, 0, j, 0))
    prev = lambda part: pl.BlockSpec((A_GROUPS, None, dilation, halo, LANES),
                                     lambda b, j: (part, b, 0, jnp.maximum(j * hb - 1, 0), 0))
    nxt = lambda part: pl.BlockSpec((A_GROUPS, None, dilation, halo, LANES),
                                    lambda b, j: (part, b, 0, jnp.minimum((j + 1) * hb, L // halo - 1), 0))
    out_spec = pl.BlockSpec((A_GROUPS, ATT_TOKENS, LANES), lambda b, j: (0, b * nj + j, 0))
    return pl.pallas_call(
        functools.partial(_attn_kernel, dilation=dilation, rows=rows),
        grid=(batch, nj),
        in_specs=[cur(0), prev(1), cur(1), nxt(1), prev(2), cur(2), nxt(2),
                  pl.BlockSpec(bias_tbl.shape, lambda b, j: (0, 0, 0, 0))],
        out_specs=[out_spec, out_spec],
        out_shape=[jax.ShapeDtypeStruct((A_GROUPS, T, LANES), F32)] * 2,
        compiler_params=_params(("parallel", "parallel")),
        name=f"dilated_attn_d{dilation}",
    )(a_split, a_split, a_split, a_split, a_split, a_split, a_split, bias_tbl)


def _scan_constants():
    t = np.arange(ROW_BLOCK)[:, None]
    r = np.arange(ROW_BLOCK)[None, :]
    same = t // CHUNK == r // CHUNK
    totals = np.repeat(np.arange(ROW_BLOCK // CHUNK), TOTAL_ROWS)[:, None] == r // CHUNK
    prefix = same & (r <= t)
    suffix = same & (r >= t)
    state_f = np.concatenate([same & (r > t), totals], axis=0)
    state_b = np.concatenate([same & (r < t), totals], axis=0)
    as_bf16 = lambda m: jnp.asarray(m.astype(np.float32), BF16)
    return as_bf16(prefix), as_bf16(suffix), as_bf16(state_f), as_bf16(state_b)


def _intra_masks():
    t = lax.broadcasted_iota(jnp.int32, (CHUNK, CHUNK), 0)
    s = lax.broadcasted_iota(jnp.int32, (CHUNK, CHUNK), 1)
    tb, sb = t // SUB, s // SUB
    return (tb == sb) & (s <= t), tb > sb, (tb == sb) & (s > t)


N_OPERANDS = 5


def _decayed_operands(cum_ref, q_rows, k_rows, op_ref, backward):
    per_chunk = CHUNK // SUB
    half = SUB // 2
    for b in range(ROW_BLOCK // SUB):
        r0 = b * SUB
        rows = slice(r0, r0 + SUB)
        row = lambda i: cum_ref[i:i + 1, :]
        if backward:
            mid, r_out = row(r0 + half), row(r0)
            r_in = None if b % per_chunk == per_chunk - 1 else row(r0 + SUB)
        else:
            mid, r_out = row(r0 + half - 1), row(r0 + SUB - 1)
            r_in = None if b % per_chunk == 0 else row(r0 - 1)
        p = cum_ref[rows, :]
        q, k = q_rows(rows), k_rows(rows)
        d_mid = p - mid
        q_in = q * jnp.exp2(p if r_in is None else p - r_in)
        q_dec = q_in if r_in is None else q_in * jnp.exp2(r_in)
        for i, val in enumerate((q * jnp.exp2(d_mid), k * jnp.exp2(-d_mid), q_in, k * jnp.exp2(r_out - p), q_dec)):
            op_ref[i, rows, :] = val.astype(BF16)


def _chunk_output(ci, opf, opb, v, stf, stb, masks):
    m_df, m_of, m_db = masks
    rows = slice(ci * CHUNK, (ci + 1) * CHUNK)
    a = jnp.where(m_df, _dot_nt(opf[0, rows, :], opf[1, rows, :]),
                  jnp.where(m_of, _dot_nt(opf[2, rows, :], opf[3, rows, :]),
                            jnp.where(m_db, _dot_nt(opb[0, rows, :], opb[1, rows, :]),
                                      _dot_nt(opb[2, rows, :], opb[3, rows, :]))))
    return _dot(a.astype(BF16), v) + _dot_nt(opf[4, rows, :], stf) + _dot_nt(opb[4, rows, :], stb)


def _state_pass(n_heads, nv, get_lg, get_k, v_ref, msel, out_ref, st_ref, kdec_scr, etot_scr, reverse):
    n_chunks = ROW_BLOCK // CHUNK
    order = range(n_chunks - 1, -1, -1) if reverse else range(n_chunks)
    for h in range(n_heads):
        lg = get_lg(h)
        d = _dot_sel(msel, lg)
        kdec_scr[h] = (get_k(h, lg) * jnp.exp2(d[0:ROW_BLOCK])).astype(BF16)
        etot_scr[h] = jnp.exp2(d[ROW_BLOCK:])
    for h in range(n_heads):
        ds = {}
        for ci in order:
            rows = slice(ci * CHUNK, (ci + 1) * CHUNK)
            ds[ci] = _dot_tn(_head_value(v_ref, h, rows, nv), kdec_scr[h, rows, :])
        st = st_ref[h]
        for ci in order:
            out_ref[ci, h] = st.astype(out_ref.dtype)
            st = st * etot_scr[h, ci * TOTAL_ROWS:ci * TOTAL_ROWS + 1, :] + ds[ci]
        st_ref[h] = st


def _head_value(v_ref, h, rows, nv):
    if nv == 1:
        return v_ref[h, rows, :]
    return jnp.concatenate([v_ref[nv * h + k, rows, :] for k in range(nv)], axis=1)


def _gla_log_gates(lr, up_pad, bias_pad, lg_scr):
    z = _dot_f32(lr, up_pad) + bias_pad
    lg = _log_sigmoid(z) * (LOG2E / B_GATE_NORMALIZER)
    for g in range(2 * B_HEADS):
        lg_scr[g] = lg[:, g * LANES:(g + 1) * LANES]


def _state_kernel_c(lg_ref, v_ref, msel_ref, out_ref, st_ref, kdec_scr, etot_scr, *, reverse):
    @pl.when(pl.program_id(1) == 0)
    def _():
        st_ref[...] = jnp.zeros_like(st_ref)

    _state_pass(C_HEADS, 1, lambda h: lg_ref[h], lambda h, lg: 1.0 - jnp.exp2(lg), v_ref, msel_ref[...],
                out_ref, st_ref, kdec_scr, etot_scr, reverse)


def _state_kernel_b(lr_ref, k_ref, v_ref, up_ref, bias_ref, msel_ref, out_ref, st_ref, lg_scr, kdec_scr,
                    etot_scr, *, reverse):
    @pl.when(pl.program_id(1) == 0)
    def _():
        st_ref[...] = jnp.zeros_like(st_ref)

    _gla_log_gates(lr_ref[...], up_ref[...], bias_ref[...], lg_scr)
    gate0 = B_HEADS if reverse else 0
    _state_pass(B_HEADS, B_VAL_DIM // LANES, lambda h: lg_scr[gate0 + h], lambda h, lg: k_ref[h].astype(F32),
                v_ref, msel_ref[...], out_ref, st_ref, kdec_scr, etot_scr, reverse)


def _seq_block_index(reverse, blocks_per_seq):
    if reverse:
        return lambda b, n: b * blocks_per_seq + (blocks_per_seq - 1 - n)
    return lambda b, n: b * blocks_per_seq + n


def _states_c(lg, v, msel, batch, seq, reverse):
    T = v.shape[1]
    nb = seq // ROW_BLOCK
    cpb = ROW_BLOCK // CHUNK
    blk = _seq_block_index(reverse, nb)
    gate_part = 1 if reverse else 0
    return pl.pallas_call(
        functools.partial(_state_kernel_c, reverse=reverse),
        grid=(batch, nb),
        in_specs=[
            pl.BlockSpec((C_HEADS, ROW_BLOCK, LANES), lambda b, n: (gate_part, blk(b, n), 0)),
            pl.BlockSpec((C_HEADS, ROW_BLOCK, LANES), lambda b, n: (0, blk(b, n), 0)),
            pl.BlockSpec(msel.shape, lambda b, n: (0, 0)),
        ],
        out_specs=pl.BlockSpec((cpb, C_HEADS, C_HEAD_DIM, C_HEAD_DIM), lambda b, n: (blk(b, n), 0, 0, 0)),
        out_shape=jax.ShapeDtypeStruct((T // CHUNK, C_HEADS, C_HEAD_DIM, C_HEAD_DIM), BF16),
        scratch_shapes=[pltpu.VMEM((C_HEADS, C_HEAD_DIM, C_HEAD_DIM), F32),
                        pltpu.VMEM((C_HEADS, ROW_BLOCK, LANES), BF16),
                        pltpu.VMEM((C_HEADS, cpb * TOTAL_ROWS, LANES), F32)],
        compiler_params=_params(("parallel", "arbitrary")),
        name="hgrn_states_bwd" if reverse else "hgrn_states_fwd",
    )(lg, v, msel)


def _states_b(lr, bqk, bv, up_pad, bias_pad, msel, batch, seq, reverse):
    T = lr.shape[0]
    nb = seq // ROW_BLOCK
    cpb = ROW_BLOCK // CHUNK
    blk = _seq_block_index(reverse, nb)
    full = lambda a: pl.BlockSpec(a.shape, lambda b, n: (0,) * a.ndim)
    return pl.pallas_call(
        functools.partial(_state_kernel_b, reverse=reverse),
        grid=(batch, nb),
        in_specs=[
            pl.BlockSpec((ROW_BLOCK, LANES), lambda b, n: (blk(b, n), 0)),
            pl.BlockSpec((B_HEADS, ROW_BLOCK, LANES), lambda b, n: (1, blk(b, n), 0)),
            pl.BlockSpec((2 * B_HEADS, ROW_BLOCK, LANES), lambda b, n: (0, blk(b, n), 0)),
            full(up_pad), full(bias_pad), full(msel),
        ],
        out_specs=pl.BlockSpec((cpb, B_HEADS, B_VAL_DIM, B_KEY_DIM), lambda b, n: (blk(b, n), 0, 0, 0)),
        out_shape=jax.ShapeDtypeStruct((T // CHUNK, B_HEADS, B_VAL_DIM, B_KEY_DIM), BF16),
        scratch_shapes=[pltpu.VMEM((B_HEADS, B_VAL_DIM, B_KEY_DIM), F32),
                        pltpu.VMEM((2 * B_HEADS, ROW_BLOCK, LANES), F32),
                        pltpu.VMEM((B_HEADS, ROW_BLOCK, LANES), BF16),
                        pltpu.VMEM((B_HEADS, cpb * TOTAL_ROWS, LANES), F32)],
        compiler_params=_params(("parallel", "arbitrary")),
        name="gla_states_bwd" if reverse else "gla_states_fwd",
    )(lr, bqk, bv, up_pad, bias_pad, msel)


def _project_and_norm(y_scr, wout_ref, x_ref, lng_ref, lnb_ref, out_ref):
    y = jnp.concatenate([y_scr[g] for g in range(MIX_GROUPS)], axis=1)
    z = DEEPNORM_ALPHA * x_ref[...] + _dot(y, wout_ref[...])
    out_ref[...] = _layer_norm(z, lng_ref[...], lnb_ref[...])


def _rms_gain(o, gain):
    return o * lax.rsqrt(jnp.mean(o * o, axis=-1, keepdims=True) + NORM_EPS) * gain


def _scan_heads(n_heads, nv, lg_of, q_rows_of, k_rows_of, v_ref, sf_ref, sb_ref, mpre, msuf, cum_scr, op_scr,
                finish):
    masks = _intra_masks()

    def step(i, carry):
        heads = [i * HEADS_PER_STEP + j for j in range(HEADS_PER_STEP)]
        for j, h in enumerate(heads):
            cum_scr[j, 0] = _dot_sel(mpre, lg_of(h, 0))
            cum_scr[j, 1] = _dot_sel(msuf, lg_of(h, 1))
        for j, h in enumerate(heads):
            for d in range(2):
                _decayed_operands(cum_scr.at[j, d], q_rows_of(h), k_rows_of(h, d), op_scr.at[j, d], d == 1)
        for j, h in enumerate(heads):
            for ci in range(ROW_BLOCK // CHUNK):
                rows = slice(ci * CHUNK, (ci + 1) * CHUNK)
                o = _chunk_output(ci, op_scr.at[j, 0], op_scr.at[j, 1], _head_value(v_ref, h, rows, nv),
                                  sf_ref[ci, h], sb_ref[ci, h], masks)
                finish(h, rows, o)
        return carry

    lax.fori_loop(0, n_heads // HEADS_PER_STEP, step, 0)


def _out_kernel_c(q_ref, i_ref, gate_ref, lgf_ref, lgb_ref, sf_ref, sb_ref, mpre_ref, msuf_ref, gain_ref,
                  wout_ref, x_ref, lng_ref, lnb_ref, out_ref, y_scr, cum_scr, op_scr):
    lg_refs = (lgf_ref, lgb_ref)

    def finish(h, rows, o):
        y = _rms_gain(o, gain_ref[h]) * _silu(gate_ref[h, rows, :].astype(F32))
        y_scr[h, rows, :] = y.astype(BF16)

    _scan_heads(C_HEADS, 1,
                lambda h, d: lg_refs[d][h],
                lambda h: lambda rows: q_ref[h, rows, :].astype(F32),
                lambda h, d: lambda rows: 1.0 - jnp.exp2(lg_refs[d][h, rows, :]),
                i_ref, sf_ref, sb_ref, mpre_ref[...], msuf_ref[...], cum_scr, op_scr, finish)
    _project_and_norm(y_scr, wout_ref, x_ref, lng_ref, lnb_ref, out_ref)


def _out_kernel_ab(o1_ref, o2_ref, o3_ref, l1_ref, l2_ref, l3_ref, qk_ref, v_ref, gate_ref, lr_ref,
                   sf_ref, sb_ref, mf_ref, mb_ref, up_ref, gbias_ref, gain_ref,
                   wout_ref, x_ref, lng_ref, lnb_ref, out_ref, y_scr, lg_scr, cum_scr, op_scr):
    for g in range(A_GROUPS):
        l1, l2, l3 = l1_ref[g], l2_ref[g], l3_ref[g]
        m = jnp.maximum(jnp.maximum(l1, l2), l3)
        e1, e2, e3 = jnp.exp(l1 - m), jnp.exp(l2 - m), jnp.exp(l3 - m)
        num = e1 * o1_ref[g].astype(F32) + e2 * o2_ref[g].astype(F32) + e3 * o3_ref[g].astype(F32)
        oa = num / (e1 + e2 + e3)
        y_scr[g] = (oa * _silu(gate_ref[g].astype(F32))).astype(BF16)

    _gla_log_gates(lr_ref[...], up_ref[...], gbias_ref[...], lg_scr)
    nv = B_VAL_DIM // LANES

    def finish(h, rows, o):
        gain = jnp.concatenate([gain_ref[nv * h + k] for k in range(nv)], axis=1)
        o = _rms_gain(o, gain)
        for kk in range(nv):
            grp = A_GROUPS + nv * h + kk
            y = o[:, kk * LANES:(kk + 1) * LANES] * _silu(gate_ref[grp, rows, :].astype(F32))
            y_scr[grp, rows, :] = y.astype(BF16)

    _scan_heads(B_HEADS, nv,
                lambda h, d: lg_scr[d * B_HEADS + h],
                lambda h: lambda rows: qk_ref[h, rows, :].astype(F32),
                lambda h, d: lambda rows: qk_ref[B_HEADS + h, rows, :].astype(F32),
                v_ref, sf_ref, sb_ref, mf_ref[...], mb_ref[...], cum_scr, op_scr, finish)
    _project_and_norm(y_scr, wout_ref, x_ref, lng_ref, lnb_ref, out_ref)


def _scan_scratch(tb):
    return [pltpu.VMEM((HEADS_PER_STEP, 2, tb, LANES), F32),
            pltpu.VMEM((HEADS_PER_STEP, 2, N_OPERANDS, tb, LANES), BF16)]


def _row_specs(tb):
    grp = lambda n, part=0: pl.BlockSpec((n, tb, LANES), lambda i: (part, i, 0))
    full = lambda a: pl.BlockSpec(a.shape, lambda i: (0,) * a.ndim)
    rows = lambda width: pl.BlockSpec((tb, width), lambda i: (i, 0))
    return grp, full, rows


def _output_c(cq, ci_, gate, lg, sf, sb, mf, mb, gain, wout, x2d, lng, lnb):
    T = x2d.shape[0]
    tb = ROW_BLOCK
    cpb = tb // CHUNK
    grp, full, rows = _row_specs(tb)
    st = pl.BlockSpec((cpb, C_HEADS, C_HEAD_DIM, C_HEAD_DIM), lambda i: (i, 0, 0, 0))
    return pl.pallas_call(
        _out_kernel_c,
        grid=(T // tb,),
        in_specs=[grp(C_HEADS), grp(C_HEADS), grp(C_HEADS), grp(C_HEADS, 0), grp(C_HEADS, 1), st, st,
                  full(mf), full(mb), full(gain), full(wout), rows(D_MODEL), full(lng), full(lnb)],
        out_specs=rows(D_MODEL),
        out_shape=jax.ShapeDtypeStruct((T, D_MODEL), F32),
        scratch_shapes=[pltpu.VMEM((MIX_GROUPS, tb, LANES), BF16)] + _scan_scratch(tb),
        compiler_params=_params(("parallel",)),
        name="hgrn_output",
    )(cq, ci_, gate, lg, lg, sf, sb, mf, mb, gain, wout, x2d, lng, lnb)


def _output_ab(attn, bqk, bv, gate, lr, sf, sb, mf, mb, up_pad, gbias_pad, gain, wout, x2d, lng, lnb):
    T = x2d.shape[0]
    tb = ROW_BLOCK
    cpb = tb // CHUNK
    grp, full, rows = _row_specs(tb)
    st = pl.BlockSpec((cpb, B_HEADS, B_VAL_DIM, B_KEY_DIM), lambda i: (i, 0, 0, 0))
    (o1, l1), (o2, l2), (o3, l3) = attn
    return pl.pallas_call(
        _out_kernel_ab,
        grid=(T // tb,),
        in_specs=[grp(A_GROUPS)] * 6 + [grp(2 * B_HEADS), grp(8), grp(MIX_GROUPS), rows(LANES), st, st,
                                        full(mf), full(mb), full(up_pad), full(gbias_pad), full(gain),
                                        full(wout), rows(D_MODEL), full(lng), full(lnb)],
        out_specs=rows(D_MODEL),
        out_shape=jax.ShapeDtypeStruct((T, D_MODEL), F32),
        scratch_shapes=[pltpu.VMEM((MIX_GROUPS, tb, LANES), BF16),
                        pltpu.VMEM((2 * B_HEADS, tb, LANES), F32)] + _scan_scratch(tb),
        compiler_params=_params(("parallel",)),
        name="ab_output",
    )(o1, o2, o3, l1, l2, l3, bqk, bv, gate, lr, sf, sb, mf, mb, up_pad, gbias_pad, gain, wout, x2d, lng, lnb)


def _layer_ab(x2d, batch, seq, w_in, gate_up, gate_bias, norm_gain, w_out, bias_tbls, lng, lnb, consts):
    mf, mb, msel_f, msel_b = consts
    sizes = (3 * A_WIDTH + 2 * B_K_WIDTH + B_V_WIDTH, 2 * B_GATE_RANK, MIX_WIDTH)
    o0, o1 = sizes[0], sizes[0] + sizes[1]
    w_main = jnp.concatenate([w_in[:, :o0], w_in[:, o1:]], axis=1).astype(BF16)
    w_lr = jnp.pad(w_in[:, o0:o1], ((0, 0), (0, LANES - sizes[1]))).astype(BF16)
    up = gate_up.astype(F32)
    up_pad = jnp.zeros((LANES, 2 * B_K_WIDTH), F32)
    up_pad = up_pad.at[0:B_GATE_RANK, 0:B_K_WIDTH].set(up[0])
    up_pad = up_pad.at[B_GATE_RANK:2 * B_GATE_RANK, B_K_WIDTH:].set(up[1])
    gbias_pad = gate_bias.astype(F32).reshape(1, 2 * B_K_WIDTH)

    *a_splits, bqk, bv, gate, lr = _inproj_ab(x2d, w_main, w_lr, batch, seq)
    attn = [_attention_pattern(a, tbl, d) for a, tbl, d in zip(a_splits, bias_tbls, DILATIONS)]
    sf = _states_b(lr, bqk, bv, up_pad, gbias_pad, msel_f, batch, seq, reverse=False)
    sb = _states_b(lr, bqk, bv, up_pad, gbias_pad, msel_b, batch, seq, reverse=True)
    gain = norm_gain.astype(F32).reshape(B_V_WIDTH // LANES, 1, LANES)
    wout = w_out.astype(BF16)
    return _output_ab(attn, bqk, bv, gate, lr, sf, sb, mf, mb, up_pad, gbias_pad, gain, wout, x2d, lng, lnb)


def _layer_c(x2d, batch, seq, w_in, lower_bounds, layer_idx, norm_gain, w_out, lng, lnb, consts):
    mf, mb, msel_f, msel_b = consts
    cq, ci_, gate, lg = _inproj_c(x2d, w_in.astype(BF16), lower_bounds.astype(F32), layer_idx)
    sf = _states_c(lg, ci_, msel_f, batch, seq, reverse=False)
    sb = _states_c(lg, ci_, msel_b, batch, seq, reverse=True)
    gain = norm_gain.astype(F32).reshape(C_HEADS, 1, LANES)
    wout = w_out.astype(BF16)
    return _output_c(cq, ci_, gate, lg, sf, sb, mf, mb, gain, wout, x2d, lng, lnb)


def kernel(x, w_in_ab, gla_gate_up, gla_gate_bias, gla_norm, w_out_ab, w_in_c, hgrn_lower_bounds, hgrn_norm,
           w_out_c, rel_bias, ln_gain, ln_bias):
    batch, seq, _ = x.shape
    assert seq % ATT_TOKENS == 0 and seq % ROW_BLOCK == 0 and (batch * seq) % PROJ_ROWS == 0
    consts = _scan_constants()
    bias_tbls = [_bias_tables(rel_bias, d) for d in DILATIONS]
    x2d = x.astype(F32).reshape(batch * seq, D_MODEL)
    for layer in range(DEPTH):
        lng = ln_gain[layer].astype(F32).reshape(1, D_MODEL)
        lnb = ln_bias[layer].astype(F32).reshape(1, D_MODEL)
        if layer % 2 == 0:
            e = layer // 2
            x2d = _layer_ab(x2d, batch, seq, w_in_ab[e], gla_gate_up[e], gla_gate_bias[e], gla_norm[e],
                            w_out_ab[e], bias_tbls, lng, lnb, consts)
        else:
            o = layer // 2
            x2d = _layer_c(x2d, batch, seq, w_in_c[o], hgrn_lower_bounds, layer, hgrn_norm[o], w_out_c[o],
                           lng, lnb, consts)
    return x2d.reshape(batch, seq, D_MODEL).astype(x.dtype)
```

```python
import functools

import numpy as np
import jax
import jax.numpy as jnp
from jax import lax
from jax.experimental import pallas as pl
from jax.experimental.pallas import tpu as pltpu

F32 = jnp.float32
BF16 = jnp.bfloat16

LANES = 128
VMEM_LIMIT = 56 * 1024 * 1024

D_MODEL = 1024
DEPTH = 4
A_HEADS = 8
A_HEAD_DIM = 64
A_WIDTH = A_HEADS * A_HEAD_DIM
A_GROUPS = A_WIDTH // LANES
DILATIONS = (1, 4, 16)
A_HALF_STEPS = 64
REL_BUCKETS = 32
REL_MAX_DISTANCE = 1024
B_HEADS = 4
B_KEY_DIM = 128
B_VAL_DIM = 256
B_K_WIDTH = B_HEADS * B_KEY_DIM
B_V_WIDTH = B_HEADS * B_VAL_DIM
B_GATE_RANK = 16
B_GATE_NORMALIZER = 16.0
C_HEADS = 12
C_HEAD_DIM = 128
C_WIDTH = C_HEADS * C_HEAD_DIM
MIX_WIDTH = A_WIDTH + B_V_WIDTH
MIX_GROUPS = MIX_WIDTH // LANES
NORM_EPS = 1e-5
DEEPNORM_ALPHA = (2 * DEPTH) ** 0.25

NEG = -1e30
LOG2E = 1.4426950408889634
LN2 = 0.6931471805599453
ATT_Q_SCALE = A_HEAD_DIM ** -0.5 * LOG2E

CHUNK = 64
SUB = 32
ATT_Q = 128
ATT_K = ATT_Q + 2 * A_HALF_STEPS
ATT_TOKENS = 2048
ATT_BATCH = 8
ROW_BLOCK = 256
TOTAL_ROWS = 8
HEADS_PER_STEP = 4
PROJ_ROWS = 512


def _params(sem):
    return pltpu.CompilerParams(dimension_semantics=sem, vmem_limit_bytes=VMEM_LIMIT)


def _dot(a, b):
    return jnp.dot(a, b, preferred_element_type=F32)


def _dot_nt(a, b):
    return lax.dot_general(a, b, (((1,), (1,)), ((), ())), preferred_element_type=F32)


def _dot_tn(a, b):
    return lax.dot_general(a, b, (((0,), (0,)), ((), ())), preferred_element_type=F32)


def _split2(x):
    hi = x.astype(BF16)
    lo = (x - hi.astype(F32)).astype(BF16)
    return hi, lo


def _dot_sel(m01, x):
    n = x.shape[1]
    hi, lo = _split2(x)
    both = _dot(m01, jnp.concatenate([hi, lo], axis=1))
    return both[:, :n] + both[:, n:]


def _dot_f32(a, b):
    ah, al = _split2(a)
    bh, bl = _split2(b)
    return _dot(ah, bh) + _dot(ah, bl) + _dot(al, bh)


def _sigmoid(z):
    return 1.0 / (1.0 + jnp.exp(-z))


def _silu(z):
    return z * _sigmoid(z)


def _log_sigmoid(z):
    return jnp.minimum(z, 0.0) - jnp.log(1.0 + jnp.exp(-jnp.abs(z)))


def _layer_norm(z, g, b):
    mu = jnp.mean(z, axis=-1, keepdims=True)
    zc = z - mu
    var = jnp.mean(zc * zc, axis=-1, keepdims=True)
    return zc * lax.rsqrt(var + NORM_EPS) * g + b


def _store_groups(acc, out_ref, g0):
    for k in range(acc.shape[1] // LANES):
        out_ref[g0 + k] = acc[:, k * LANES:(k + 1) * LANES].astype(out_ref.dtype)


def _inproj_ab_kernel(x_ref, w_ref, wlr_ref, a1_ref, a4_ref, a16_ref, bqk_ref, bv_ref, gate_ref, lr_ref, nat_scr):
    xb = x_ref[...].astype(BF16)
    step = 4 * LANES
    tm = x_ref.shape[0]
    for c in range(0, 3 * A_WIDTH, step):
        acc = _dot(xb, w_ref[:, c:c + step])
        if c == 0:
            acc = acc * ATT_Q_SCALE
        g0 = c // LANES
        for k in range(A_GROUPS):
            nat_scr[k] = acc[:, k * LANES:(k + 1) * LANES]
        for d, ref in zip(DILATIONS, (a1_ref, a4_ref, a16_ref)):
            for k in range(A_GROUPS):
                for r in range(d):
                    src = nat_scr[k] if d == 1 else nat_scr[k, pl.ds(r, tm // d, stride=d), :]
                    ref[g0 + k, r] = src.astype(BF16)
    base = 3 * A_WIDTH
    for c in range(0, 2 * B_K_WIDTH, step):
        acc = _dot(xb, w_ref[:, base + c:base + c + step])
        if c < B_K_WIDTH:
            acc = acc * (B_KEY_DIM ** -0.5)
        _store_groups(acc, bqk_ref, c // LANES)
    base += 2 * B_K_WIDTH
    for c in range(0, B_V_WIDTH, step):
        _store_groups(_dot(xb, w_ref[:, base + c:base + c + step]), bv_ref, c // LANES)
    base += B_V_WIDTH
    for c in range(0, MIX_WIDTH, step):
        _store_groups(_dot(xb, w_ref[:, base + c:base + c + step]), gate_ref, c // LANES)
    lr_ref[...] = _dot(xb, wlr_ref[...])


def _inproj_ab(x2d, w_main, w_lr, batch, seq):
    T = x2d.shape[0]
    tm = PROJ_ROWS
    n_main = w_main.shape[1]
    nbs = seq // tm
    grp = lambda n: pl.BlockSpec((n, tm, LANES), lambda i: (0, i, 0))
    split = lambda d: pl.BlockSpec((3 * A_GROUPS, None, d, tm // d, LANES),
                                   lambda i: (0, i // nbs, 0, i % nbs, 0))
    split_shape = lambda d: jax.ShapeDtypeStruct((3 * A_GROUPS, batch, d, seq // d, LANES), BF16)
    return pl.pallas_call(
        _inproj_ab_kernel,
        grid=(T // tm,),
        in_specs=[
            pl.BlockSpec((tm, D_MODEL), lambda i: (i, 0)),
            pl.BlockSpec((D_MODEL, n_main), lambda i: (0, 0)),
            pl.BlockSpec((D_MODEL, LANES), lambda i: (0, 0)),
        ],
        out_specs=[split(d) for d in DILATIONS] + [grp(8), grp(8), grp(MIX_GROUPS),
                                                   pl.BlockSpec((tm, LANES), lambda i: (i, 0))],
        scratch_shapes=[pltpu.VMEM((A_GROUPS, tm, LANES), F32)],
        out_shape=[split_shape(d) for d in DILATIONS] + [
            jax.ShapeDtypeStruct((8, T, LANES), BF16),
            jax.ShapeDtypeStruct((8, T, LANES), BF16),
            jax.ShapeDtypeStruct((MIX_GROUPS, T, LANES), BF16),
            jax.ShapeDtypeStruct((T, LANES), F32),
        ],
        compiler_params=_params(("parallel",)),
        name="inproj_ab",
    )(x2d, w_main, w_lr)


def _inproj_c_kernel(x_ref, w_ref, lbnd_ref, q_ref, i_ref, gate_ref, lg_ref, *, layer_idx):
    xb = x_ref[...].astype(BF16)
    lbnd = lbnd_ref[...]
    e = jnp.exp(lbnd - jnp.max(lbnd, axis=0, keepdims=True))
    sm = e / jnp.sum(e, axis=0, keepdims=True)
    lb = jnp.sum(sm[1:layer_idx + 1], axis=0, keepdims=True)
    step = 4 * LANES
    for c in range(0, C_WIDTH, step):
        q = _dot(xb, w_ref[:, c:c + step])
        _store_groups(_silu(q) * (C_HEAD_DIM ** -0.5), q_ref, c // LANES)
    for d in range(2):
        base = (1 + d) * C_WIDTH
        for c in range(0, C_WIDTH, step):
            z = _dot(xb, w_ref[:, base + c:base + c + step])
            lbc = lb[:, c:c + step]
            f = lbc + (1.0 - lbc) * _sigmoid(z)
            _store_groups(jnp.log2(f), lg_ref, (d * C_WIDTH + c) // LANES)
    for c in range(0, C_WIDTH, step):
        _store_groups(_dot(xb, w_ref[:, 3 * C_WIDTH + c:3 * C_WIDTH + c + step]), i_ref, c // LANES)
    for c in range(0, C_WIDTH, step):
        _store_groups(_dot(xb, w_ref[:, 4 * C_WIDTH + c:4 * C_WIDTH + c + step]), gate_ref, c // LANES)


def _inproj_c(x2d, w, lower_bounds, layer_idx):
    T = x2d.shape[0]
    tm = PROJ_ROWS // 2
    grp = lambda n: pl.BlockSpec((n, tm, LANES), lambda i: (0, i, 0))
    return pl.pallas_call(
        functools.partial(_inproj_c_kernel, layer_idx=layer_idx),
        grid=(T // tm,),
        in_specs=[
            pl.BlockSpec((tm, D_MODEL), lambda i: (i, 0)),
            pl.BlockSpec((D_MODEL, 5 * C_WIDTH), lambda i: (0, 0)),
            pl.BlockSpec((DEPTH, C_WIDTH), lambda i: (0, 0)),
        ],
        out_specs=[grp(C_HEADS), grp(C_HEADS), grp(C_HEADS), grp(2 * C_HEADS)],
        out_shape=[
            jax.ShapeDtypeStruct((C_HEADS, T, LANES), BF16),
            jax.ShapeDtypeStruct((C_HEADS, T, LANES), BF16),
            jax.ShapeDtypeStruct((C_HEADS, T, LANES), BF16),
            jax.ShapeDtypeStruct((2 * C_HEADS, T, LANES), F32),
        ],
        compiler_params=_params(("parallel",)),
        name="inproj_c",
    )(x2d, w, lower_bounds)


def _t5_bucket(rel):
    half = REL_BUCKETS // 2
    max_exact = half // 2
    n = np.abs(rel)
    large = max_exact + (np.log(np.maximum(n, 1) / max_exact)
                         / np.log(REL_MAX_DISTANCE / max_exact) * (half - max_exact)).astype(np.int32)
    large = np.minimum(large, half - 1)
    return np.where(rel > 0, half, 0) + np.where(n < max_exact, n, large)


def _bias_tables(rel_bias, dilation):
    offs = np.arange(-A_HALF_STEPS, A_HALF_STEPS + 1)
    vals = rel_bias.astype(F32)[_t5_bucket(offs * dilation)] * LOG2E
    period = ATT_K + ATT_Q
    ext = jnp.concatenate([vals, jnp.full((period - vals.shape[0], A_HEADS), NEG, F32)], axis=0).T
    flat = jnp.broadcast_to(ext[:, None, :], (A_HEADS, ATT_Q, period)).reshape(A_HEADS, ATT_Q * period)
    tbl = flat[:, :ATT_Q * (period - 1)].reshape(A_HEADS, ATT_Q, period - 1)[:, :, :ATT_K]
    col = np.arange(ATT_K)
    before, after = col < A_HALF_STEPS, col >= ATT_K - A_HALF_STEPS
    variants = []
    for mask in (np.zeros_like(before), before, after, before | after):
        t = jnp.where(mask[None, None, :], NEG, tbl).reshape(A_GROUPS, 2, ATT_Q, ATT_K)
        variants.append(jnp.transpose(t, (0, 2, 1, 3)).reshape(A_GROUPS, ATT_Q, 2 * ATT_K))
    return jnp.stack(variants, axis=0)


def _attn_kernel(q_ref, kp_ref, kc_ref, kn_ref, vp_ref, vc_ref, vn_ref, bias_ref, o_ref, lse_ref, *,
                 dilation, rows):
    j = pl.program_id(1)
    first = (j == 0).astype(jnp.int32)
    last = (j == pl.num_programs(1) - 1).astype(jnp.int32)
    halo = A_HALF_STEPS
    n_sub = rows // ATT_Q
    low_kv = lax.broadcasted_iota(jnp.int32, (ATT_K, LANES), 1) < A_HEAD_DIM
    low_q = lax.broadcasted_iota(jnp.int32, (ATT_Q, LANES), 1) < A_HEAD_DIM
    zero = jnp.zeros((ATT_K, LANES), BF16)
    one = jnp.ones((ATT_K, LANES), BF16)

    def window(prev_ref, cur_ref, next_ref, g, r, i):
        lo, hi = i * ATT_Q - halo, i * ATT_Q - halo + ATT_K
        parts = [prev_ref[g, r]] if lo < 0 else []
        parts.append(cur_ref[g, r, max(lo, 0):min(hi, rows), :])
        if hi > rows:
            parts.append(next_ref[g, r])
        return parts[0] if len(parts) == 1 else jnp.concatenate(parts, axis=0)

    units = [(r, i) for r in range(dilation) for i in range(n_sub)]

    def scores(g, r, i):
        variant = (first if i == 0 else 0) + (2 * last if i == n_sub - 1 else 0)
        bias = bias_ref[variant, g]
        q2 = q_ref[g, r, i * ATT_Q:(i + 1) * ATT_Q, :]
        k2 = window(kp_ref, kc_ref, kn_ref, g, r, i)
        s0 = _dot_nt(q2, jnp.where(low_kv, k2, zero)) + bias[:, :ATT_K]
        s1 = _dot_nt(q2, jnp.where(low_kv, zero, k2)) + bias[:, ATT_K:]
        return s0, s1

    def softmax(s):
        m = jnp.max(s, axis=1, keepdims=True)
        return m, jnp.exp2(s - m).astype(BF16)

    def finish(g, r, i, m0, p0, m1, p1):
        v2 = window(vp_ref, vc_ref, vn_ref, g, r, i)
        r0 = _dot(p0, jnp.where(low_kv, v2, one))
        r1 = _dot(p1, jnp.where(low_kv, one, v2))
        pv = jnp.where(low_q, r0, r1)
        denom = pltpu.roll(jnp.where(low_q, r1, r0), A_HEAD_DIM, axis=1)
        m = jnp.where(low_q, m0, m1)
        start = r + dilation * i * ATT_Q
        dst = pl.ds(start, ATT_Q) if dilation == 1 else pl.ds(start, ATT_Q, stride=dilation)
        o_ref[g, dst, :] = pv / denom
        lse_ref[g, dst, :] = LN2 * (m + jnp.log2(denom))

    def group_body(g, carry):
        for u0 in range(0, len(units), ATT_BATCH):
            batch = units[u0:u0 + ATT_BATCH]
            s = [scores(g, r, i) for r, i in batch]
            sm = [(softmax(s0), softmax(s1)) for s0, s1 in s]
            for (r, i), ((m0, p0), (m1, p1)) in zip(batch, sm):
                finish(g, r, i, m0, p0, m1, p1)
        return carry

    lax.fori_loop(0, A_GROUPS, group_body, 0)


def _attention_pattern(a_split, bias_tbl, dilation):
    _, batch, _, L, _ = a_split.shape
    T = batch * L * dilation
    rows = ATT_TOKENS // dilation
    halo = A_HALF_STEPS
    hb = rows // halo
    nj = L // rows
    cur = lambda part: pl.BlockSpec((A_GROUPS, None, dilation, rows, LANES), lambda b, j: (part, b, 0, j, 0))
    prev = lambda part: pl.BlockSpec((A_GROUPS, None, dilation, halo, LANES),
                                     lambda b, j: (part, b, 0, jnp.maximum(j * hb - 1, 0), 0))
    nxt = lambda part: pl.BlockSpec((A_GROUPS, None, dilation, halo, LANES),
                                    lambda b, j: (part, b, 0, jnp.minimum((j + 1) * hb, L // halo - 1), 0))
    out_spec = pl.BlockSpec((A_GROUPS, ATT_TOKENS, LANES), lambda b, j: (0, b * nj + j, 0))
    return pl.pallas_call(
        functools.partial(_attn_kernel, dilation=dilation, rows=rows),
        grid=(batch, nj),
        in_specs=[cur(0), prev(1), cur(1), nxt(1), prev(2), cur(2), nxt(2),
                  pl.BlockSpec(bias_tbl.shape, lambda b, j: (0, 0, 0, 0))],
        out_specs=[out_spec, out_spec],
        out_shape=[jax.ShapeDtypeStruct((A_GROUPS, T, LANES), F32)] * 2,
        compiler_params=_params(("parallel", "parallel")),
        name=f"dilated_attn_d{dilation}",
    )(a_split, a_split, a_split, a_split, a_split, a_split, a_split, bias_tbl)


def _scan_constants():
    t = np.arange(ROW_BLOCK)[:, None]
    r = np.arange(ROW_BLOCK)[None, :]
    same = t // CHUNK == r // CHUNK
    totals = np.repeat(np.arange(ROW_BLOCK // CHUNK), TOTAL_ROWS)[:, None] == r // CHUNK
    prefix = same & (r <= t)
    suffix = same & (r >= t)
    state_b = np.concatenate([same & (r < t), totals], axis=0)
    as_bf16 = lambda m: jnp.asarray(m.astype(np.float32), BF16)
    return as_bf16(prefix), as_bf16(suffix), as_bf16(state_b)


def _intra_masks():
    t = lax.broadcasted_iota(jnp.int32, (CHUNK, CHUNK), 0)
    s = lax.broadcasted_iota(jnp.int32, (CHUNK, CHUNK), 1)
    tb, sb = t // SUB, s // SUB
    return (tb == sb) & (s <= t), tb > sb, (tb == sb) & (s > t)


N_OPERANDS = 6


def _decayed_operands(cum_ref, q_rows, k_rows, op_ref, backward):
    per_chunk = CHUNK // SUB
    half = SUB // 2
    for b in range(ROW_BLOCK // SUB):
        r0 = b * SUB
        rows = slice(r0, r0 + SUB)
        row = lambda i: cum_ref[i:i + 1, :]
        if backward:
            mid, r_out = row(r0 + half), row(r0)
            r_in = None if b % per_chunk == per_chunk - 1 else row(r0 + SUB)
        else:
            mid, r_out = row(r0 + half - 1), row(r0 + SUB - 1)
            r_in = None if b % per_chunk == 0 else row(r0 - 1)
        p = cum_ref[rows, :]
        q, k = q_rows(rows), k_rows(rows)
        d_mid = p - mid
        q_in = q * jnp.exp2(p if r_in is None else p - r_in)
        q_dec = q_in if r_in is None else q_in * jnp.exp2(r_in)
        k_out = k * jnp.exp2(r_out - p)
        vals = [q * jnp.exp2(d_mid), k * jnp.exp2(-d_mid), q_in, k_out, q_dec]
        if not backward:
            chunk_end = row((b // per_chunk + 1) * CHUNK - 1)
            vals.append(k_out if b % per_chunk == per_chunk - 1 else k_out * jnp.exp2(chunk_end - r_out))
        for i, val in enumerate(vals):
            op_ref[i, rows, :] = val.astype(BF16)


def _intra_matrix(ci, opf, opb, masks):
    m_df, m_of, m_db = masks
    rows = slice(ci * CHUNK, (ci + 1) * CHUNK)
    a = jnp.where(m_df, _dot_nt(opf[0, rows, :], opf[1, rows, :]),
                  jnp.where(m_of, _dot_nt(opf[2, rows, :], opf[3, rows, :]),
                            jnp.where(m_db, _dot_nt(opb[0, rows, :], opb[1, rows, :]),
                                      _dot_nt(opb[2, rows, :], opb[3, rows, :]))))
    return a.astype(BF16)


def _chunk_output(ci, a, opf, opb, v, stf, stb):
    rows = slice(ci * CHUNK, (ci + 1) * CHUNK)
    q_dec = jnp.concatenate([opf[4, rows, :], opb[4, rows, :]], axis=1)
    return _dot(a, v) + _dot_nt(q_dec, jnp.concatenate([stf, stb], axis=1))


def _state_pass(n_heads, nv, get_lg, get_k, v_ref, msel, out_ref, st_ref, kdec_scr, etot_scr):
    order = range(ROW_BLOCK // CHUNK - 1, -1, -1)
    for h in range(n_heads):
        lg = get_lg(h)
        d = _dot_sel(msel, lg)
        kdec_scr[h] = (get_k(h, lg) * jnp.exp2(d[0:ROW_BLOCK])).astype(BF16)
        etot_scr[h] = jnp.exp2(d[ROW_BLOCK:])
    for h in range(n_heads):
        ds = {}
        for ci in order:
            rows = slice(ci * CHUNK, (ci + 1) * CHUNK)
            ds[ci] = _dot_tn(_head_value(v_ref, h, rows, nv), kdec_scr[h, rows, :])
        st = st_ref[h]
        for ci in order:
            out_ref[ci, h] = st.astype(out_ref.dtype)
            st = st * etot_scr[h, ci * TOTAL_ROWS:ci * TOTAL_ROWS + 1, :] + ds[ci]
        st_ref[h] = st


def _head_value(v_ref, h, rows, nv):
    if nv == 1:
        return v_ref[h, rows, :]
    return jnp.concatenate([v_ref[nv * h + k, rows, :] for k in range(nv)], axis=1)


def _gla_log_gates(lr, up_pad, bias_pad, lg_scr):
    z = _dot_f32(lr, up_pad) + bias_pad
    lg = _log_sigmoid(z) * (LOG2E / B_GATE_NORMALIZER)
    for g in range(2 * B_HEADS):
        lg_scr[g] = lg[:, g * LANES:(g + 1) * LANES]


def _state_kernel_c(lg_ref, v_ref, msel_ref, out_ref, st_ref, kdec_scr, etot_scr):
    @pl.when(pl.program_id(1) == 0)
    def _():
        st_ref[...] = jnp.zeros_like(st_ref)

    _state_pass(C_HEADS, 1, lambda h: lg_ref[h], lambda h, lg: 1.0 - jnp.exp2(lg), v_ref, msel_ref[...],
                out_ref, st_ref, kdec_scr, etot_scr)


def _state_kernel_b(lr_ref, k_ref, v_ref, up_ref, bias_ref, msel_ref, out_ref, st_ref, lg_scr, kdec_scr,
                    etot_scr):
    @pl.when(pl.program_id(1) == 0)
    def _():
        st_ref[...] = jnp.zeros_like(st_ref)

    _gla_log_gates(lr_ref[...], up_ref[...], bias_ref[...], lg_scr)
    _state_pass(B_HEADS, B_VAL_DIM // LANES, lambda h: lg_scr[B_HEADS + h], lambda h, lg: k_ref[h].astype(F32),
                v_ref, msel_ref[...], out_ref, st_ref, kdec_scr, etot_scr)


def _reversed_block_index(blocks_per_seq):
    return lambda b, n: b * blocks_per_seq + (blocks_per_seq - 1 - n)


def _states_c(lg, v, msel, batch, seq):
    T = v.shape[1]
    nb = seq // ROW_BLOCK
    cpb = ROW_BLOCK // CHUNK
    blk = _reversed_block_index(nb)
    return pl.pallas_call(
        _state_kernel_c,
        grid=(batch, nb),
        in_specs=[
            pl.BlockSpec((C_HEADS, ROW_BLOCK, LANES), lambda b, n: (1, blk(b, n), 0)),
            pl.BlockSpec((C_HEADS, ROW_BLOCK, LANES), lambda b, n: (0, blk(b, n), 0)),
            pl.BlockSpec(msel.shape, lambda b, n: (0, 0)),
        ],
        out_specs=pl.BlockSpec((cpb, C_HEADS, C_HEAD_DIM, C_HEAD_DIM), lambda b, n: (blk(b, n), 0, 0, 0)),
        out_shape=jax.ShapeDtypeStruct((T // CHUNK, C_HEADS, C_HEAD_DIM, C_HEAD_DIM), BF16),
        scratch_shapes=[pltpu.VMEM((C_HEADS, C_HEAD_DIM, C_HEAD_DIM), F32),
                        pltpu.VMEM((C_HEADS, ROW_BLOCK, LANES), BF16),
                        pltpu.VMEM((C_HEADS, cpb * TOTAL_ROWS, LANES), F32)],
        compiler_params=_params(("parallel", "arbitrary")),
        name="hgrn_states_bwd",
    )(lg, v, msel)


def _states_b(lr, bqk, bv, up_pad, bias_pad, msel, batch, seq):
    T = lr.shape[0]
    nb = seq // ROW_BLOCK
    cpb = ROW_BLOCK // CHUNK
    blk = _reversed_block_index(nb)
    full = lambda a: pl.BlockSpec(a.shape, lambda b, n: (0,) * a.ndim)
    return pl.pallas_call(
        _state_kernel_b,
        grid=(batch, nb),
        in_specs=[
            pl.BlockSpec((ROW_BLOCK, LANES), lambda b, n: (blk(b, n), 0)),
            pl.BlockSpec((B_HEADS, ROW_BLOCK, LANES), lambda b, n: (1, blk(b, n), 0)),
            pl.BlockSpec((2 * B_HEADS, ROW_BLOCK, LANES), lambda b, n: (0, blk(b, n), 0)),
            full(up_pad), full(bias_pad), full(msel),
        ],
        out_specs=pl.BlockSpec((cpb, B_HEADS, B_VAL_DIM, B_KEY_DIM), lambda b, n: (blk(b, n), 0, 0, 0)),
        out_shape=jax.ShapeDtypeStruct((T // CHUNK, B_HEADS, B_VAL_DIM, B_KEY_DIM), BF16),
        scratch_shapes=[pltpu.VMEM((B_HEADS, B_VAL_DIM, B_KEY_DIM), F32),
                        pltpu.VMEM((2 * B_HEADS, ROW_BLOCK, LANES), F32),
                        pltpu.VMEM((B_HEADS, ROW_BLOCK, LANES), BF16),
                        pltpu.VMEM((B_HEADS, cpb * TOTAL_ROWS, LANES), F32)],
        compiler_params=_params(("parallel", "arbitrary")),
        name="gla_states_bwd",
    )(lr, bqk, bv, up_pad, bias_pad, msel)


def _project_and_norm(y_scr, wout_ref, x_ref, lng_ref, lnb_ref, out_ref):
    y = jnp.concatenate([y_scr[g] for g in range(MIX_GROUPS)], axis=1)
    z = DEEPNORM_ALPHA * x_ref[...] + _dot(y, wout_ref[...])
    out_ref[...] = _layer_norm(z, lng_ref[...], lnb_ref[...])


def _rms_gain(o, gain):
    return o * lax.rsqrt(jnp.mean(o * o, axis=-1, keepdims=True) + NORM_EPS) * gain


def _scan_heads(n_heads, nv, lg_of, q_rows_of, k_rows_of, v_ref, st_ref, sb_ref, mpre, msuf, scan_scr, finish):
    masks = _intra_masks()

    cum_scr, op_scr, a_scr, ds_scr, stf_scr = scan_scr

    @pl.when(pl.program_id(1) == 0)
    def _():
        st_ref[...] = jnp.zeros_like(st_ref)

    def step(i, carry):
        heads = [i * HEADS_PER_STEP + j for j in range(HEADS_PER_STEP)]
        for j, h in enumerate(heads):
            cum_scr[j, 0] = _dot_sel(mpre, lg_of(h, 0))
            cum_scr[j, 1] = _dot_sel(msuf, lg_of(h, 1))
        for j, h in enumerate(heads):
            for d in range(2):
                _decayed_operands(cum_scr.at[j, d], q_rows_of(h), k_rows_of(h, d), op_scr.at[j, d], d == 1)
        units = [(j, h, ci) for j, h in enumerate(heads) for ci in range(ROW_BLOCK // CHUNK)]
        chunk_rows = lambda ci: slice(ci * CHUNK, (ci + 1) * CHUNK)
        for j, h, ci in units:
            a_scr[j, ci] = _intra_matrix(ci, op_scr.at[j, 0], op_scr.at[j, 1], masks)
            ds_scr[j, ci] = _dot_tn(_head_value(v_ref, h, chunk_rows(ci), nv),
                                    op_scr[j, 0, N_OPERANDS - 1, chunk_rows(ci), :])
        for j, h in enumerate(heads):
            st = st_ref[h]
            for ci in range(ROW_BLOCK // CHUNK):
                stf_scr[j, ci] = st.astype(BF16)
                total = cum_scr[j, 0, (ci + 1) * CHUNK - 1:(ci + 1) * CHUNK, :]
                st = st * jnp.exp2(total) + ds_scr[j, ci]
            st_ref[h] = st
        for j, h, ci in units:
            o = _chunk_output(ci, a_scr[j, ci], op_scr.at[j, 0], op_scr.at[j, 1],
                              _head_value(v_ref, h, chunk_rows(ci), nv), stf_scr[j, ci], sb_ref[ci, h])
            finish(h, chunk_rows(ci), o)
        return carry

    lax.fori_loop(0, n_heads // HEADS_PER_STEP, step, 0)


def _out_kernel_c(q_ref, i_ref, gate_ref, lgf_ref, lgb_ref, sb_ref, mpre_ref, msuf_ref, gain_ref,
                  wout_ref, x_ref, lng_ref, lnb_ref, out_ref, y_scr, st_scr, *scan_scr):
    lg_refs = (lgf_ref, lgb_ref)

    def finish(h, rows, o):
        y = _rms_gain(o, gain_ref[h]) * _silu(gate_ref[h, rows, :].astype(F32))
        y_scr[h, rows, :] = y.astype(BF16)

    _scan_heads(C_HEADS, 1,
                lambda h, d: lg_refs[d][h],
                lambda h: lambda rows: q_ref[h, rows, :].astype(F32),
                lambda h, d: lambda rows: 1.0 - jnp.exp2(lg_refs[d][h, rows, :]),
                i_ref, st_scr, sb_ref, mpre_ref[...], msuf_ref[...], scan_scr, finish)
    _project_and_norm(y_scr, wout_ref, x_ref, lng_ref, lnb_ref, out_ref)


def _out_kernel_ab(o1_ref, o2_ref, o3_ref, l1_ref, l2_ref, l3_ref, qk_ref, v_ref, gate_ref, lr_ref,
                   sb_ref, mf_ref, mb_ref, up_ref, gbias_ref, gain_ref,
                   wout_ref, x_ref, lng_ref, lnb_ref, out_ref, y_scr, lg_scr, st_scr, *scan_scr):
    for g in range(A_GROUPS):
        l1, l2, l3 = l1_ref[g], l2_ref[g], l3_ref[g]
        m = jnp.maximum(jnp.maximum(l1, l2), l3)
        e1, e2, e3 = jnp.exp(l1 - m), jnp.exp(l2 - m), jnp.exp(l3 - m)
        num = e1 * o1_ref[g].astype(F32) + e2 * o2_ref[g].astype(F32) + e3 * o3_ref[g].astype(F32)
        oa = num / (e1 + e2 + e3)
        y_scr[g] = (oa * _silu(gate_ref[g].astype(F32))).astype(BF16)

    _gla_log_gates(lr_ref[...], up_ref[...], gbias_ref[...], lg_scr)
    nv = B_VAL_DIM // LANES

    def finish(h, rows, o):
        gain = jnp.concatenate([gain_ref[nv * h + k] for k in range(nv)], axis=1)
        o = _rms_gain(o, gain)
        for kk in range(nv):
            grp = A_GROUPS + nv * h + kk
            y = o[:, kk * LANES:(kk + 1) * LANES] * _silu(gate_ref[grp, rows, :].astype(F32))
            y_scr[grp, rows, :] = y.astype(BF16)

    _scan_heads(B_HEADS, nv,
                lambda h, d: lg_scr[d * B_HEADS + h],
                lambda h: lambda rows: qk_ref[h, rows, :].astype(F32),
                lambda h, d: lambda rows: qk_ref[B_HEADS + h, rows, :].astype(F32),
                v_ref, st_scr, sb_ref, mf_ref[...], mb_ref[...], scan_scr, finish)
    _project_and_norm(y_scr, wout_ref, x_ref, lng_ref, lnb_ref, out_ref)


def _scan_scratch(tb, v_dim, k_dim):
    cpb = tb // CHUNK
    return [pltpu.VMEM((HEADS_PER_STEP, 2, tb, LANES), F32),
            pltpu.VMEM((HEADS_PER_STEP, 2, N_OPERANDS, tb, LANES), BF16),
            pltpu.VMEM((HEADS_PER_STEP, cpb, CHUNK, CHUNK), BF16),
            pltpu.VMEM((HEADS_PER_STEP, cpb, v_dim, k_dim), F32),
            pltpu.VMEM((HEADS_PER_STEP, cpb, v_dim, k_dim), BF16)]


def _row_specs(tb, nb):
    grp = lambda n, part=0: pl.BlockSpec((n, tb, LANES), lambda b, i: (part, b * nb + i, 0))
    full = lambda a: pl.BlockSpec(a.shape, lambda b, i: (0,) * a.ndim)
    rows = lambda width: pl.BlockSpec((tb, width), lambda b, i: (b * nb + i, 0))
    return grp, full, rows


def _output_c(cq, ci_, gate, lg, sb, mf, mb, gain, wout, x2d, lng, lnb, batch, seq):
    T = x2d.shape[0]
    tb = ROW_BLOCK
    nb = seq // tb
    cpb = tb // CHUNK
    grp, full, rows = _row_specs(tb, nb)
    st = pl.BlockSpec((cpb, C_HEADS, C_HEAD_DIM, C_HEAD_DIM), lambda b, i: (b * nb + i, 0, 0, 0))
    return pl.pallas_call(
        _out_kernel_c,
        grid=(batch, nb),
        in_specs=[grp(C_HEADS), grp(C_HEADS), grp(C_HEADS), grp(C_HEADS, 0), grp(C_HEADS, 1), st,
                  full(mf), full(mb), full(gain), full(wout), rows(D_MODEL), full(lng), full(lnb)],
        out_specs=rows(D_MODEL),
        out_shape=jax.ShapeDtypeStruct((T, D_MODEL), F32),
        scratch_shapes=[pltpu.VMEM((MIX_GROUPS, tb, LANES), BF16),
                        pltpu.VMEM((C_HEADS, C_HEAD_DIM, C_HEAD_DIM), F32)]
        + _scan_scratch(tb, C_HEAD_DIM, C_HEAD_DIM),
        compiler_params=_params(("parallel", "arbitrary")),
        name="hgrn_output",
    )(cq, ci_, gate, lg, lg, sb, mf, mb, gain, wout, x2d, lng, lnb)


def _output_ab(attn, bqk, bv, gate, lr, sb, mf, mb, up_pad, gbias_pad, gain, wout, x2d, lng, lnb, batch, seq):
    T = x2d.shape[0]
    tb = ROW_BLOCK
    nb = seq // tb
    cpb = tb // CHUNK
    grp, full, rows = _row_specs(tb, nb)
    st = pl.BlockSpec((cpb, B_HEADS, B_VAL_DIM, B_KEY_DIM), lambda b, i: (b * nb + i, 0, 0, 0))
    (o1, l1), (o2, l2), (o3, l3) = attn
    return pl.pallas_call(
        _out_kernel_ab,
        grid=(batch, nb),
        in_specs=[grp(A_GROUPS)] * 6 + [grp(2 * B_HEADS), grp(8), grp(MIX_GROUPS), rows(LANES), st,
                                        full(mf), full(mb), full(up_pad), full(gbias_pad), full(gain),
                                        full(wout), rows(D_MODEL), full(lng), full(lnb)],
        out_specs=rows(D_MODEL),
        out_shape=jax.ShapeDtypeStruct((T, D_MODEL), F32),
        scratch_shapes=[pltpu.VMEM((MIX_GROUPS, tb, LANES), BF16),
                        pltpu.VMEM((2 * B_HEADS, tb, LANES), F32),
                        pltpu.VMEM((B_HEADS, B_VAL_DIM, B_KEY_DIM), F32)]
        + _scan_scratch(tb, B_VAL_DIM, B_KEY_DIM),
        compiler_params=_params(("parallel", "arbitrary")),
        name="ab_output",
    )(o1, o2, o3, l1, l2, l3, bqk, bv, gate, lr, sb, mf, mb, up_pad, gbias_pad, gain, wout, x2d, lng, lnb)


def _layer_ab(x2d, batch, seq, w_in, gate_up, gate_bias, norm_gain, w_out, bias_tbls, lng, lnb, consts):
    mf, mb, msel = consts
    sizes = (3 * A_WIDTH + 2 * B_K_WIDTH + B_V_WIDTH, 2 * B_GATE_RANK, MIX_WIDTH)
    o0, o1 = sizes[0], sizes[0] + sizes[1]
    w_main = jnp.concatenate([w_in[:, :o0], w_in[:, o1:]], axis=1).astype(BF16)
    w_lr = jnp.pad(w_in[:, o0:o1], ((0, 0), (0, LANES - sizes[1]))).astype(BF16)
    up = gate_up.astype(F32)
    up_pad = jnp.zeros((LANES, 2 * B_K_WIDTH), F32)
    up_pad = up_pad.at[0:B_GATE_RANK, 0:B_K_WIDTH].set(up[0])
    up_pad = up_pad.at[B_GATE_RANK:2 * B_GATE_RANK, B_K_WIDTH:].set(up[1])
    gbias_pad = gate_bias.astype(F32).reshape(1, 2 * B_K_WIDTH)

    *a_splits, bqk, bv, gate, lr = _inproj_ab(x2d, w_main, w_lr, batch, seq)
    attn = [_attention_pattern(a, tbl, d) for a, tbl, d in zip(a_splits, bias_tbls, DILATIONS)]
    sb = _states_b(lr, bqk, bv, up_pad, gbias_pad, msel, batch, seq)
    gain = norm_gain.astype(F32).reshape(B_V_WIDTH // LANES, 1, LANES)
    wout = w_out.astype(BF16)
    return _output_ab(attn, bqk, bv, gate, lr, sb, mf, mb, up_pad, gbias_pad, gain, wout, x2d, lng, lnb,
                      batch, seq)


def _layer_c(x2d, batch, seq, w_in, lower_bounds, layer_idx, norm_gain, w_out, lng, lnb, consts):
    mf, mb, msel = consts
    cq, ci_, gate, lg = _inproj_c(x2d, w_in.astype(BF16), lower_bounds.astype(F32), layer_idx)
    sb = _states_c(lg, ci_, msel, batch, seq)
    gain = norm_gain.astype(F32).reshape(C_HEADS, 1, LANES)
    wout = w_out.astype(BF16)
    return _output_c(cq, ci_, gate, lg, sb, mf, mb, gain, wout, x2d, lng, lnb, batch, seq)


def kernel(x, w_in_ab, gla_gate_up, gla_gate_bias, gla_norm, w_out_ab, w_in_c, hgrn_lower_bounds, hgrn_norm,
           w_out_c, rel_bias, ln_gain, ln_bias):
    batch, seq, _ = x.shape
    assert seq % ATT_TOKENS == 0 and seq % ROW_BLOCK == 0 and (batch * seq) % PROJ_ROWS == 0
    consts = _scan_constants()
    bias_tbls = [_bias_tables(rel_bias, d) for d in DILATIONS]
    x2d = x.astype(F32).reshape(batch * seq, D_MODEL)
    for layer in range(DEPTH):
        lng = ln_gain[layer].astype(F32).reshape(1, D_MODEL)
        lnb = ln_bias[layer].astype(F32).reshape(1, D_MODEL)
        if layer % 2 == 0:
            e = layer // 2
            x2d = _layer_ab(x2d, batch, seq, w_in_ab[e], gla_gate_up[e], gla_gate_bias[e], gla_norm[e],
                            w_out_ab[e], bias_tbls, lng, lnb, consts)
        else:
            o = layer // 2
            x2d = _layer_c(x2d, batch, seq, w_in_c[o], hgrn_lower_bounds, layer, hgrn_norm[o], w_out_c[o],
                           lng, lnb, consts)
    return x2d.reshape(batch, seq, D_MODEL).astype(x.dtype)
```

```python
import functools

import numpy as np
import jax
import jax.numpy as jnp
from jax import lax
from jax.experimental import pallas as pl
from jax.experimental.pallas import tpu as pltpu

F32 = jnp.float32
BF16 = jnp.bfloat16

LANES = 128
VMEM_LIMIT = 56 * 1024 * 1024

D_MODEL = 1024
DEPTH = 4
A_HEADS = 8
A_HEAD_DIM = 64
A_WIDTH = A_HEADS * A_HEAD_DIM
A_GROUPS = A_WIDTH // LANES
DILATIONS = (1, 4, 16)
A_HALF_STEPS = 64
REL_BUCKETS = 32
REL_MAX_DISTANCE = 1024
B_HEADS = 4
B_KEY_DIM = 128
B_VAL_DIM = 256
B_K_WIDTH = B_HEADS * B_KEY_DIM
B_V_WIDTH = B_HEADS * B_VAL_DIM
B_GATE_RANK = 16
B_GATE_NORMALIZER = 16.0
C_HEADS = 12
C_HEAD_DIM = 128
C_WIDTH = C_HEADS * C_HEAD_DIM
MIX_WIDTH = A_WIDTH + B_V_WIDTH
MIX_GROUPS = MIX_WIDTH // LANES
NORM_EPS = 1e-5
DEEPNORM_ALPHA = (2 * DEPTH) ** 0.25

NEG = -1e30
LOG2E = 1.4426950408889634
ATT_Q_SCALE = A_HEAD_DIM ** -0.5 * LOG2E

CHUNK = 64
SUB = 32
ATT_Q = 128
ATT_K = ATT_Q + 2 * A_HALF_STEPS
ATT_TOKENS = 2048
ATT_BATCH = 8
ROW_BLOCK = 256
TOTAL_ROWS = 8
HEADS_PER_STEP = 6
PROJ_ROWS = 512


def _params(sem):
    return pltpu.CompilerParams(dimension_semantics=sem, vmem_limit_bytes=VMEM_LIMIT)


def _dot(a, b):
    return jnp.dot(a, b, preferred_element_type=F32)


def _dot_nt(a, b):
    return lax.dot_general(a, b, (((1,), (1,)), ((), ())), preferred_element_type=F32)


def _dot_tn(a, b):
    return lax.dot_general(a, b, (((0,), (0,)), ((), ())), preferred_element_type=F32)


def _split2(x):
    hi = x.astype(BF16)
    lo = (x - hi.astype(F32)).astype(BF16)
    return hi, lo


def _dot_sel(m01, x):
    n = x.shape[1]
    hi, lo = _split2(x)
    both = _dot(m01, jnp.concatenate([hi, lo], axis=1))
    return both[:, :n] + both[:, n:]


def _dot_f32(a, b):
    ah, al = _split2(a)
    bh, bl = _split2(b)
    return _dot(ah, bh) + _dot(ah, bl) + _dot(al, bh)


def _sigmoid(z):
    return 1.0 / (1.0 + jnp.exp(-z))


def _silu(z):
    return z * _sigmoid(z)


def _log_sigmoid(z):
    return jnp.minimum(z, 0.0) - jnp.log(1.0 + jnp.exp(-jnp.abs(z)))


def _layer_norm(z, g, b):
    mu = jnp.mean(z, axis=-1, keepdims=True)
    zc = z - mu
    var = jnp.mean(zc * zc, axis=-1, keepdims=True)
    return zc * lax.rsqrt(var + NORM_EPS) * g + b


def _store_groups(acc, out_ref, g0):
    for k in range(acc.shape[1] // LANES):
        out_ref[g0 + k] = acc[:, k * LANES:(k + 1) * LANES].astype(out_ref.dtype)


def _inproj_ab_kernel(x_ref, w_ref, wlr_ref, a1_ref, a4_ref, a16_ref, bqk_ref, bv_ref, gate_ref, lr_ref,
                      nat_scr, split_scr):
    xb = x_ref[...].astype(BF16)
    step = 4 * LANES
    tm = x_ref.shape[0]
    ratio = DILATIONS[1]
    for c in range(0, 3 * A_WIDTH, step):
        acc = _dot(xb, w_ref[:, c:c + step])
        if c == 0:
            acc = acc * ATT_Q_SCALE
        g0 = c // LANES
        for k in range(A_GROUPS):
            slab = acc[:, k * LANES:(k + 1) * LANES]
            nat_scr[k] = slab
            a1_ref[g0 + k, 0] = slab.astype(BF16)
        for k in range(A_GROUPS):
            for r in range(ratio):
                part = nat_scr[k, pl.ds(r, tm // ratio, stride=ratio), :]
                split_scr[k, r] = part
                a4_ref[g0 + k, r] = part.astype(BF16)
            for r in range(ratio):
                for r2 in range(ratio):
                    part = split_scr[k, r, pl.ds(r2, tm // ratio // ratio, stride=ratio), :]
                    a16_ref[g0 + k, r + ratio * r2] = part.astype(BF16)
    base = 3 * A_WIDTH
    for c in range(0, 2 * B_K_WIDTH, step):
        acc = _dot(xb, w_ref[:, base + c:base + c + step])
        if c < B_K_WIDTH:
            acc = acc * (B_KEY_DIM ** -0.5)
        _store_groups(acc, bqk_ref, c // LANES)
    base += 2 * B_K_WIDTH
    for c in range(0, B_V_WIDTH, step):
        _store_groups(_dot(xb, w_ref[:, base + c:base + c + step]), bv_ref, c // LANES)
    base += B_V_WIDTH
    for c in range(0, MIX_WIDTH, step):
        _store_groups(_dot(xb, w_ref[:, base + c:base + c + step]), gate_ref, c // LANES)
    lr_ref[...] = _dot(xb, wlr_ref[...])


def _inproj_ab(x2d, w_main, w_lr, batch, seq):
    T = x2d.shape[0]
    tm = PROJ_ROWS
    n_main = w_main.shape[1]
    nbs = seq // tm
    grp = lambda n: pl.BlockSpec((n, tm, LANES), lambda i: (0, i, 0))
    split = lambda d: pl.BlockSpec((3 * A_GROUPS, None, d, tm // d, LANES),
                                   lambda i: (0, i // nbs, 0, i % nbs, 0))
    split_shape = lambda d: jax.ShapeDtypeStruct((3 * A_GROUPS, batch, d, seq // d, LANES), BF16)
    return pl.pallas_call(
        _inproj_ab_kernel,
        grid=(T // tm,),
        in_specs=[
            pl.BlockSpec((tm, D_MODEL), lambda i: (i, 0)),
            pl.BlockSpec((D_MODEL, n_main), lambda i: (0, 0)),
            pl.BlockSpec((D_MODEL, LANES), lambda i: (0, 0)),
        ],
        out_specs=[split(d) for d in DILATIONS] + [grp(8), grp(8), grp(MIX_GROUPS),
                                                   pl.BlockSpec((tm, LANES), lambda i: (i, 0))],
        scratch_shapes=[pltpu.VMEM((A_GROUPS, tm, LANES), F32),
                        pltpu.VMEM((A_GROUPS, DILATIONS[1], tm // DILATIONS[1], LANES), F32)],
        out_shape=[split_shape(d) for d in DILATIONS] + [
            jax.ShapeDtypeStruct((8, T, LANES), BF16),
            jax.ShapeDtypeStruct((8, T, LANES), BF16),
            jax.ShapeDtypeStruct((MIX_GROUPS, T, LANES), BF16),
            jax.ShapeDtypeStruct((T, LANES), F32),
        ],
        compiler_params=_params(("parallel",)),
        name="inproj_ab",
    )(x2d, w_main, w_lr)


def _inproj_c_kernel(x_ref, w_ref, lbnd_ref, q_ref, i_ref, gate_ref, lg_ref, *, layer_idx):
    xb = x_ref[...].astype(BF16)
    lbnd = lbnd_ref[...]
    e = jnp.exp(lbnd - jnp.max(lbnd, axis=0, keepdims=True))
    sm = e / jnp.sum(e, axis=0, keepdims=True)
    lb = jnp.sum(sm[1:layer_idx + 1], axis=0, keepdims=True)
    step = 4 * LANES
    for c in range(0, C_WIDTH, step):
        q = _dot(xb, w_ref[:, c:c + step])
        _store_groups(_silu(q) * (C_HEAD_DIM ** -0.5), q_ref, c // LANES)
    for d in range(2):
        base = (1 + d) * C_WIDTH
        for c in range(0, C_WIDTH, step):
            z = _dot(xb, w_ref[:, base + c:base + c + step])
            lbc = lb[:, c:c + step]
            f = lbc + (1.0 - lbc) * _sigmoid(z)
            _store_groups(jnp.log2(f), lg_ref, (d * C_WIDTH + c) // LANES)
    for c in range(0, C_WIDTH, step):
        _store_groups(_dot(xb, w_ref[:, 3 * C_WIDTH + c:3 * C_WIDTH + c + step]), i_ref, c // LANES)
    for c in range(0, C_WIDTH, step):
        _store_groups(_dot(xb, w_ref[:, 4 * C_WIDTH + c:4 * C_WIDTH + c + step]), gate_ref, c // LANES)


def _inproj_c(x2d, w, lower_bounds, layer_idx):
    T = x2d.shape[0]
    tm = PROJ_ROWS // 2
    grp = lambda n: pl.BlockSpec((n, tm, LANES), lambda i: (0, i, 0))
    return pl.pallas_call(
        functools.partial(_inproj_c_kernel, layer_idx=layer_idx),
        grid=(T // tm,),
        in_specs=[
            pl.BlockSpec((tm, D_MODEL), lambda i: (i, 0)),
            pl.BlockSpec((D_MODEL, 5 * C_WIDTH), lambda i: (0, 0)),
            pl.BlockSpec((DEPTH, C_WIDTH), lambda i: (0, 0)),
        ],
        out_specs=[grp(C_HEADS), grp(C_HEADS), grp(C_HEADS), grp(2 * C_HEADS)],
        out_shape=[
            jax.ShapeDtypeStruct((C_HEADS, T, LANES), BF16),
            jax.ShapeDtypeStruct((C_HEADS, T, LANES), BF16),
            jax.ShapeDtypeStruct((C_HEADS, T, LANES), BF16),
            jax.ShapeDtypeStruct((2 * C_HEADS, T, LANES), F32),
        ],
        compiler_params=_params(("parallel",)),
        name="inproj_c",
    )(x2d, w, lower_bounds)


def _t5_bucket(rel):
    half = REL_BUCKETS // 2
    max_exact = half // 2
    n = np.abs(rel)
    large = max_exact + (np.log(np.maximum(n, 1) / max_exact)
                         / np.log(REL_MAX_DISTANCE / max_exact) * (half - max_exact)).astype(np.int32)
    large = np.minimum(large, half - 1)
    return np.where(rel > 0, half, 0) + np.where(n < max_exact, n, large)


def _bias_tables(rel_bias, dilation):
    offs = np.arange(-A_HALF_STEPS, A_HALF_STEPS + 1)
    vals = rel_bias.astype(F32)[_t5_bucket(offs * dilation)] * LOG2E
    period = ATT_K + ATT_Q
    ext = jnp.concatenate([vals, jnp.full((period - vals.shape[0], A_HEADS), NEG, F32)], axis=0).T
    flat = jnp.broadcast_to(ext[:, None, :], (A_HEADS, ATT_Q, period)).reshape(A_HEADS, ATT_Q * period)
    tbl = flat[:, :ATT_Q * (period - 1)].reshape(A_HEADS, ATT_Q, period - 1)[:, :, :ATT_K]
    col = np.arange(ATT_K)
    before, after = col < A_HALF_STEPS, col >= ATT_K - A_HALF_STEPS
    variants = []
    for mask in (np.zeros_like(before), before, after, before | after):
        t = jnp.where(mask[None, None, :], NEG, tbl).reshape(A_GROUPS, 2, ATT_Q, ATT_K)
        variants.append(jnp.transpose(t, (0, 2, 1, 3)).reshape(A_GROUPS, ATT_Q, 2 * ATT_K))
    return jnp.stack(variants, axis=0)


def _attn_kernel(q_ref, kp_ref, kc_ref, kn_ref, vp_ref, vc_ref, vn_ref, bias_ref, o_ref, lse_ref, *,
                 dilation, rows):
    j = pl.program_id(1)
    first = (j == 0).astype(jnp.int32)
    last = (j == pl.num_programs(1) - 1).astype(jnp.int32)
    halo = A_HALF_STEPS
    n_sub = rows // ATT_Q
    low_kv = lax.broadcasted_iota(jnp.int32, (ATT_K, LANES), 1) < A_HEAD_DIM
    low_q = lax.broadcasted_iota(jnp.int32, (ATT_Q, LANES), 1) < A_HEAD_DIM
    zero = jnp.zeros((ATT_K, LANES), BF16)
    one = jnp.ones((ATT_K, LANES), BF16)

    def window(prev_ref, cur_ref, next_ref, g, r, i):
        lo, hi = i * ATT_Q - halo, i * ATT_Q - halo + ATT_K
        parts = [prev_ref[g, r]] if lo < 0 else []
        parts.append(cur_ref[g, r, max(lo, 0):min(hi, rows), :])
        if hi > rows:
            parts.append(next_ref[g, r])
        return parts[0] if len(parts) == 1 else jnp.concatenate(parts, axis=0)

    units = [(r, i) for r in range(dilation) for i in range(n_sub)]

    def scores(g, r, i):
        variant = (first if i == 0 else 0) + (2 * last if i == n_sub - 1 else 0)
        bias = bias_ref[variant, g]
        q2 = q_ref[g, r, i * ATT_Q:(i + 1) * ATT_Q, :]
        k2 = window(kp_ref, kc_ref, kn_ref, g, r, i)
        s0 = _dot_nt(q2, jnp.where(low_kv, k2, zero)) + bias[:, :ATT_K]
        s1 = _dot_nt(q2, jnp.where(low_kv, zero, k2)) + bias[:, ATT_K:]
        return s0, s1

    def softmax(s):
        m = jnp.max(s, axis=1, keepdims=True)
        return m, jnp.exp2(s - m).astype(BF16)

    def finish(g, r, i, m0, p0, m1, p1):
        v2 = window(vp_ref, vc_ref, vn_ref, g, r, i)
        r0 = _dot(p0, jnp.where(low_kv, v2, one))
        r1 = _dot(p1, jnp.where(low_kv, one, v2))
        pv = jnp.where(low_q, r0, r1)
        denom = pltpu.roll(jnp.where(low_q, r1, r0), A_HEAD_DIM, axis=1)
        m = jnp.where(low_q, m0, m1)
        start = r + dilation * i * ATT_Q
        dst = pl.ds(start, ATT_Q) if dilation == 1 else pl.ds(start, ATT_Q, stride=dilation)
        o_ref[g, dst, :] = pv / denom
        lse_ref[g, dst, :] = m + jnp.log2(denom)

    def group_body(g, carry):
        for u0 in range(0, len(units), ATT_BATCH):
            batch = units[u0:u0 + ATT_BATCH]
            s = [scores(g, r, i) for r, i in batch]
            sm = [(softmax(s0), softmax(s1)) for s0, s1 in s]
            for (r, i), ((m0, p0), (m1, p1)) in zip(batch, sm):
                finish(g, r, i, m0, p0, m1, p1)
        return carry

    lax.fori_loop(0, A_GROUPS, group_body, 0)


def _attention_pattern(a_split, bias_tbl, dilation):
    _, batch, _, L, _ = a_split.shape
    T = batch * L * dilation
    rows = ATT_TOKENS // dilation
    halo = A_HALF_STEPS
    hb = rows // halo
    nj = L // rows
    cur = lambda part: pl.BlockSpec((A_GROUPS, None, dilation, rows, LANES), lambda b, j: (part, b, 0, j, 0))
    prev = lambda part: pl.BlockSpec((A_GROUPS, None, dilation, halo, LANES),
                                     lambda b, j: (part, b, 0, jnp.maximum(j * hb - 1, 0), 0))
    nxt = lambda part: pl.BlockSpec((A_GROUPS, None, dilation, halo, LANES),
                                    lambda b, j: (part, b, 0, jnp.minimum((j + 1) * hb, L // halo - 1), 0))
    out_spec = pl.BlockSpec((A_GROUPS, ATT_TOKENS, LANES), lambda b, j: (0, b * nj + j, 0))
    return pl.pallas_call(
        functools.partial(_attn_kernel, dilation=dilation, rows=rows),
        grid=(batch, nj),
        in_specs=[cur(0), prev(1), cur(1), nxt(1), prev(2), cur(2), nxt(2),
                  pl.BlockSpec(bias_tbl.shape, lambda b, j: (0, 0, 0, 0))],
        out_specs=[out_spec, out_spec],
        out_shape=[jax.ShapeDtypeStruct((A_GROUPS, T, LANES), F32)] * 2,
        compiler_params=_params(("parallel", "parallel")),
        name=f"dilated_attn_d{dilation}",
    )(a_split, a_split, a_split, a_split, a_split, a_split, a_split, bias_tbl)


def _scan_constants():
    t = np.arange(ROW_BLOCK)[:, None]
    r = np.arange(ROW_BLOCK)[None, :]
    same = t // CHUNK == r // CHUNK
    totals = np.repeat(np.arange(ROW_BLOCK // CHUNK), TOTAL_ROWS)[:, None] == r // CHUNK
    prefix = same & (r <= t)
    suffix = same & (r >= t)
    state_b = np.concatenate([same & (r < t), totals], axis=0)
    as_bf16 = lambda m: jnp.asarray(m.astype(np.float32), BF16)
    return as_bf16(prefix), as_bf16(suffix), as_bf16(state_b)


def _intra_masks():
    t = lax.broadcasted_iota(jnp.int32, (CHUNK, CHUNK), 0)
    s = lax.broadcasted_iota(jnp.int32, (CHUNK, CHUNK), 1)
    tb, sb = t // SUB, s // SUB
    return (tb == sb) & (s <= t), tb > sb, (tb == sb) & (s > t)


N_OPERANDS = 6


def _decayed_operands(cum_ref, q_rows, k_rows, op_ref, backward):
    per_chunk = CHUNK // SUB
    half = SUB // 2
    for b in range(ROW_BLOCK // SUB):
        r0 = b * SUB
        rows = slice(r0, r0 + SUB)
        row = lambda i: cum_ref[i:i + 1, :]
        if backward:
            mid, r_out = row(r0 + half), row(r0)
            r_in = None if b % per_chunk == per_chunk - 1 else row(r0 + SUB)
        else:
            mid, r_out = row(r0 + half - 1), row(r0 + SUB - 1)
            r_in = None if b % per_chunk == 0 else row(r0 - 1)
        p = cum_ref[rows, :]
        q, k = q_rows(rows), k_rows(rows)
        d_mid = p - mid
        q_in = q * jnp.exp2(p if r_in is None else p - r_in)
        q_dec = q_in if r_in is None else q_in * jnp.exp2(r_in)
        k_out = k * jnp.exp2(r_out - p)
        vals = [q * jnp.exp2(d_mid), k * jnp.exp2(-d_mid), q_in, k_out, q_dec]
        if not backward:
            chunk_end = row((b // per_chunk + 1) * CHUNK - 1)
            vals.append(k_out if b % per_chunk == per_chunk - 1 else k_out * jnp.exp2(chunk_end - r_out))
        for i, val in enumerate(vals):
            op_ref[i, rows, :] = val.astype(BF16)


def _intra_matrix(ci, opf, opb, masks):
    m_df, m_of, m_db = masks
    rows = slice(ci * CHUNK, (ci + 1) * CHUNK)
    a = jnp.where(m_df, _dot_nt(opf[0, rows, :], opf[1, rows, :]),
                  jnp.where(m_of, _dot_nt(opf[2, rows, :], opf[3, rows, :]),
                            jnp.where(m_db, _dot_nt(opb[0, rows, :], opb[1, rows, :]),
                                      _dot_nt(opb[2, rows, :], opb[3, rows, :]))))
    return a.astype(BF16)


def _chunk_output(ci, a, opf, opb, v, stf, stb):
    rows = slice(ci * CHUNK, (ci + 1) * CHUNK)
    q_dec = jnp.concatenate([opf[4, rows, :], opb[4, rows, :]], axis=1)
    return _dot(a, v) + _dot_nt(q_dec, jnp.concatenate([stf, stb], axis=1))


def _state_pass(n_heads, nv, get_lg, get_k, v_ref, msel, out_ref, st_ref, kdec_scr, etot_scr):
    order = range(ROW_BLOCK // CHUNK - 1, -1, -1)
    for h in range(n_heads):
        lg = get_lg(h)
        d = _dot_sel(msel, lg)
        kdec_scr[h] = (get_k(h, lg) * jnp.exp2(d[0:ROW_BLOCK])).astype(BF16)
        etot_scr[h] = jnp.exp2(d[ROW_BLOCK:])
    for h in range(n_heads):
        ds = {}
        for ci in order:
            rows = slice(ci * CHUNK, (ci + 1) * CHUNK)
            ds[ci] = _dot_tn(_head_value(v_ref, h, rows, nv), kdec_scr[h, rows, :])
        st = st_ref[h]
        for ci in order:
            out_ref[ci, h] = st.astype(out_ref.dtype)
            st = st * etot_scr[h, ci * TOTAL_ROWS:ci * TOTAL_ROWS + 1, :] + ds[ci]
        st_ref[h] = st


def _head_value(v_ref, h, rows, nv):
    if nv == 1:
        return v_ref[h, rows, :]
    return jnp.concatenate([v_ref[nv * h + k, rows, :] for k in range(nv)], axis=1)


def _gla_log_gates(lr, up_pad, bias_pad, lg_scr):
    z = _dot_f32(lr, up_pad) + bias_pad
    lg = _log_sigmoid(z) * (LOG2E / B_GATE_NORMALIZER)
    for g in range(2 * B_HEADS):
        lg_scr[g] = lg[:, g * LANES:(g + 1) * LANES]


def _state_kernel_c(lg_ref, v_ref, msel_ref, out_ref, st_ref, kdec_scr, etot_scr):
    @pl.when(pl.program_id(1) == 0)
    def _():
        st_ref[...] = jnp.zeros_like(st_ref)

    _state_pass(C_HEADS, 1, lambda h: lg_ref[h], lambda h, lg: 1.0 - jnp.exp2(lg), v_ref, msel_ref[...],
                out_ref, st_ref, kdec_scr, etot_scr)


def _state_kernel_b(lr_ref, k_ref, v_ref, up_ref, bias_ref, msel_ref, out_ref, st_ref, lg_scr, kdec_scr,
                    etot_scr):
    @pl.when(pl.program_id(1) == 0)
    def _():
        st_ref[...] = jnp.zeros_like(st_ref)

    _gla_log_gates(lr_ref[...], up_ref[...], bias_ref[...], lg_scr)
    _state_pass(B_HEADS, B_VAL_DIM // LANES, lambda h: lg_scr[B_HEADS + h], lambda h, lg: k_ref[h].astype(F32),
                v_ref, msel_ref[...], out_ref, st_ref, kdec_scr, etot_scr)


def _reversed_block_index(blocks_per_seq):
    return lambda b, n: b * blocks_per_seq + (blocks_per_seq - 1 - n)


def _states_c(lg, v, msel, batch, seq):
    T = v.shape[1]
    nb = seq // ROW_BLOCK
    cpb = ROW_BLOCK // CHUNK
    blk = _reversed_block_index(nb)
    return pl.pallas_call(
        _state_kernel_c,
        grid=(batch, nb),
        in_specs=[
            pl.BlockSpec((C_HEADS, ROW_BLOCK, LANES), lambda b, n: (1, blk(b, n), 0)),
            pl.BlockSpec((C_HEADS, ROW_BLOCK, LANES), lambda b, n: (0, blk(b, n), 0)),
            pl.BlockSpec(msel.shape, lambda b, n: (0, 0)),
        ],
        out_specs=pl.BlockSpec((cpb, C_HEADS, C_HEAD_DIM, C_HEAD_DIM), lambda b, n: (blk(b, n), 0, 0, 0)),
        out_shape=jax.ShapeDtypeStruct((T // CHUNK, C_HEADS, C_HEAD_DIM, C_HEAD_DIM), BF16),
        scratch_shapes=[pltpu.VMEM((C_HEADS, C_HEAD_DIM, C_HEAD_DIM), F32),
                        pltpu.VMEM((C_HEADS, ROW_BLOCK, LANES), BF16),
                        pltpu.VMEM((C_HEADS, cpb * TOTAL_ROWS, LANES), F32)],
        compiler_params=_params(("parallel", "arbitrary")),
        name="hgrn_states_bwd",
    )(lg, v, msel)


def _states_b(lr, bqk, bv, up_pad, bias_pad, msel, batch, seq):
    T = lr.shape[0]
    nb = seq // ROW_BLOCK
    cpb = ROW_BLOCK // CHUNK
    blk = _reversed_block_index(nb)
    full = lambda a: pl.BlockSpec(a.shape, lambda b, n: (0,) * a.ndim)
    return pl.pallas_call(
        _state_kernel_b,
        grid=(batch, nb),
        in_specs=[
            pl.BlockSpec((ROW_BLOCK, LANES), lambda b, n: (blk(b, n), 0)),
            pl.BlockSpec((B_HEADS, ROW_BLOCK, LANES), lambda b, n: (1, blk(b, n), 0)),
            pl.BlockSpec((2 * B_HEADS, ROW_BLOCK, LANES), lambda b, n: (0, blk(b, n), 0)),
            full(up_pad), full(bias_pad), full(msel),
        ],
        out_specs=pl.BlockSpec((cpb, B_HEADS, B_VAL_DIM, B_KEY_DIM), lambda b, n: (blk(b, n), 0, 0, 0)),
        out_shape=jax.ShapeDtypeStruct((T // CHUNK, B_HEADS, B_VAL_DIM, B_KEY_DIM), BF16),
        scratch_shapes=[pltpu.VMEM((B_HEADS, B_VAL_DIM, B_KEY_DIM), F32),
                        pltpu.VMEM((2 * B_HEADS, ROW_BLOCK, LANES), F32),
                        pltpu.VMEM((B_HEADS, ROW_BLOCK, LANES), BF16),
                        pltpu.VMEM((B_HEADS, cpb * TOTAL_ROWS, LANES), F32)],
        compiler_params=_params(("parallel", "arbitrary")),
        name="gla_states_bwd",
    )(lr, bqk, bv, up_pad, bias_pad, msel)


def _project_and_norm(y_scr, wout_ref, x_ref, lng_ref, lnb_ref, out_ref):
    y = jnp.concatenate([y_scr[g] for g in range(MIX_GROUPS)], axis=1)
    z = DEEPNORM_ALPHA * x_ref[...] + _dot(y, wout_ref[...])
    out_ref[...] = _layer_norm(z, lng_ref[...], lnb_ref[...])


def _rms_gain(o, gain):
    return o * lax.rsqrt(jnp.mean(o * o, axis=-1, keepdims=True) + NORM_EPS) * gain


def _scan_heads(n_heads, nv, lg_of, q_rows_of, k_rows_of, v_ref, st_ref, sb_ref, mpre, msuf, scan_scr, finish):
    masks = _intra_masks()

    cum_scr, op_scr, a_scr, ds_scr, stf_scr = scan_scr

    @pl.when(pl.program_id(1) == 0)
    def _():
        st_ref[...] = jnp.zeros_like(st_ref)

    per_step = cum_scr.shape[0]

    def step(i, carry):
        heads = [i * per_step + j for j in range(per_step)]
        for j, h in enumerate(heads):
            cum_scr[j, 0] = _dot_sel(mpre, lg_of(h, 0))
            cum_scr[j, 1] = _dot_sel(msuf, lg_of(h, 1))
        for j, h in enumerate(heads):
            for d in range(2):
                _decayed_operands(cum_scr.at[j, d], q_rows_of(h), k_rows_of(h, d), op_scr.at[j, d], d == 1)
        units = [(j, h, ci) for j, h in enumerate(heads) for ci in range(ROW_BLOCK // CHUNK)]
        chunk_rows = lambda ci: slice(ci * CHUNK, (ci + 1) * CHUNK)
        for j, h, ci in units:
            a_scr[j, ci] = _intra_matrix(ci, op_scr.at[j, 0], op_scr.at[j, 1], masks)
            ds_scr[j, ci] = _dot_tn(_head_value(v_ref, h, chunk_rows(ci), nv),
                                    op_scr[j, 0, N_OPERANDS - 1, chunk_rows(ci), :])
        for j, h in enumerate(heads):
            st = st_ref[h]
            for ci in range(ROW_BLOCK // CHUNK):
                stf_scr[j, ci] = st.astype(BF16)
                total = cum_scr[j, 0, (ci + 1) * CHUNK - 1:(ci + 1) * CHUNK, :]
                st = st * jnp.exp2(total) + ds_scr[j, ci]
            st_ref[h] = st
        for j, h, ci in units:
            o = _chunk_output(ci, a_scr[j, ci], op_scr.at[j, 0], op_scr.at[j, 1],
                              _head_value(v_ref, h, chunk_rows(ci), nv), stf_scr[j, ci], sb_ref[ci, h])
            finish(h, chunk_rows(ci), o)
        return carry

    lax.fori_loop(0, n_heads // per_step, step, 0)


def _out_kernel_c(q_ref, i_ref, gate_ref, lgf_ref, lgb_ref, sb_ref, mpre_ref, msuf_ref, gain_ref,
                  wout_ref, x_ref, lng_ref, lnb_ref, out_ref, y_scr, st_scr, *scan_scr):
    lg_refs = (lgf_ref, lgb_ref)

    def finish(h, rows, o):
        y = _rms_gain(o, gain_ref[h]) * _silu(gate_ref[h, rows, :].astype(F32))
        y_scr[h, rows, :] = y.astype(BF16)

    _scan_heads(C_HEADS, 1,
                lambda h, d: lg_refs[d][h],
                lambda h: lambda rows: q_ref[h, rows, :].astype(F32),
                lambda h, d: lambda rows: 1.0 - jnp.exp2(lg_refs[d][h, rows, :]),
                i_ref, st_scr, sb_ref, mpre_ref[...], msuf_ref[...], scan_scr, finish)
    _project_and_norm(y_scr, wout_ref, x_ref, lng_ref, lnb_ref, out_ref)


def _out_kernel_ab(o1_ref, o2_ref, o3_ref, l1_ref, l2_ref, l3_ref, qk_ref, v_ref, gate_ref, lr_ref,
                   sb_ref, mf_ref, mb_ref, up_ref, gbias_ref, gain_ref,
                   wout_ref, x_ref, lng_ref, lnb_ref, out_ref, y_scr, lg_scr, st_scr, *scan_scr):
    for g in range(A_GROUPS):
        l1, l2, l3 = l1_ref[g], l2_ref[g], l3_ref[g]
        m = jnp.maximum(jnp.maximum(l1, l2), l3)
        e1, e2, e3 = jnp.exp2(l1 - m), jnp.exp2(l2 - m), jnp.exp2(l3 - m)
        num = e1 * o1_ref[g].astype(F32) + e2 * o2_ref[g].astype(F32) + e3 * o3_ref[g].astype(F32)
        oa = num / (e1 + e2 + e3)
        y_scr[g] = (oa * _silu(gate_ref[g].astype(F32))).astype(BF16)

    _gla_log_gates(lr_ref[...], up_ref[...], gbias_ref[...], lg_scr)
    nv = B_VAL_DIM // LANES

    def finish(h, rows, o):
        gain = jnp.concatenate([gain_ref[nv * h + k] for k in range(nv)], axis=1)
        o = _rms_gain(o, gain)
        for kk in range(nv):
            grp = A_GROUPS + nv * h + kk
            y = o[:, kk * LANES:(kk + 1) * LANES] * _silu(gate_ref[grp, rows, :].astype(F32))
            y_scr[grp, rows, :] = y.astype(BF16)

    _scan_heads(B_HEADS, nv,
                lambda h, d: lg_scr[d * B_HEADS + h],
                lambda h: lambda rows: qk_ref[h, rows, :].astype(F32),
                lambda h, d: lambda rows: qk_ref[B_HEADS + h, rows, :].astype(F32),
                v_ref, st_scr, sb_ref, mf_ref[...], mb_ref[...], scan_scr, finish)
    _project_and_norm(y_scr, wout_ref, x_ref, lng_ref, lnb_ref, out_ref)


def _scan_scratch(tb, n_heads, v_dim, k_dim):
    cpb = tb // CHUNK
    per_step = min(HEADS_PER_STEP, n_heads)
    assert n_heads % per_step == 0
    return [pltpu.VMEM((per_step, 2, tb, LANES), F32),
            pltpu.VMEM((per_step, 2, N_OPERANDS, tb, LANES), BF16),
            pltpu.VMEM((per_step, cpb, CHUNK, CHUNK), BF16),
            pltpu.VMEM((per_step, cpb, v_dim, k_dim), F32),
            pltpu.VMEM((per_step, cpb, v_dim, k_dim), BF16)]


def _row_specs(tb, nb):
    grp = lambda n, part=0: pl.BlockSpec((n, tb, LANES), lambda b, i: (part, b * nb + i, 0))
    full = lambda a: pl.BlockSpec(a.shape, lambda b, i: (0,) * a.ndim)
    rows = lambda width: pl.BlockSpec((tb, width), lambda b, i: (b * nb + i, 0))
    return grp, full, rows


def _output_c(cq, ci_, gate, lg, sb, mf, mb, gain, wout, x2d, lng, lnb, batch, seq):
    T = x2d.shape[0]
    tb = ROW_BLOCK
    nb = seq // tb
    cpb = tb // CHUNK
    grp, full, rows = _row_specs(tb, nb)
    st = pl.BlockSpec((cpb, C_HEADS, C_HEAD_DIM, C_HEAD_DIM), lambda b, i: (b * nb + i, 0, 0, 0))
    return pl.pallas_call(
        _out_kernel_c,
        grid=(batch, nb),
        in_specs=[grp(C_HEADS), grp(C_HEADS), grp(C_HEADS), grp(C_HEADS, 0), grp(C_HEADS, 1), st,
                  full(mf), full(mb), full(gain), full(wout), rows(D_MODEL), full(lng), full(lnb)],
        out_specs=rows(D_MODEL),
        out_shape=jax.ShapeDtypeStruct((T, D_MODEL), F32),
        scratch_shapes=[pltpu.VMEM((MIX_GROUPS, tb, LANES), BF16),
                        pltpu.VMEM((C_HEADS, C_HEAD_DIM, C_HEAD_DIM), F32)]
        + _scan_scratch(tb, C_HEADS, C_HEAD_DIM, C_HEAD_DIM),
        compiler_params=_params(("parallel", "arbitrary")),
        name="hgrn_output",
    )(cq, ci_, gate, lg, lg, sb, mf, mb, gain, wout, x2d, lng, lnb)


def _output_ab(attn, bqk, bv, gate, lr, sb, mf, mb, up_pad, gbias_pad, gain, wout, x2d, lng, lnb, batch, seq):
    T = x2d.shape[0]
    tb = ROW_BLOCK
    nb = seq // tb
    cpb = tb // CHUNK
    grp, full, rows = _row_specs(tb, nb)
    st = pl.BlockSpec((cpb, B_HEADS, B_VAL_DIM, B_KEY_DIM), lambda b, i: (b * nb + i, 0, 0, 0))
    (o1, l1), (o2, l2), (o3, l3) = attn
    return pl.pallas_call(
        _out_kernel_ab,
        grid=(batch, nb),
        in_specs=[grp(A_GROUPS)] * 6 + [grp(2 * B_HEADS), grp(8), grp(MIX_GROUPS), rows(LANES), st,
                                        full(mf), full(mb), full(up_pad), full(gbias_pad), full(gain),
                                        full(wout), rows(D_MODEL), full(lng), full(lnb)],
        out_specs=rows(D_MODEL),
        out_shape=jax.ShapeDtypeStruct((T, D_MODEL), F32),
        scratch_shapes=[pltpu.VMEM((MIX_GROUPS, tb, LANES), BF16),
                        pltpu.VMEM((2 * B_HEADS, tb, LANES), F32),
                        pltpu.VMEM((B_HEADS, B_VAL_DIM, B_KEY_DIM), F32)]
        + _scan_scratch(tb, B_HEADS, B_VAL_DIM, B_KEY_DIM),
        compiler_params=_params(("parallel", "arbitrary")),
        name="ab_output",
    )(o1, o2, o3, l1, l2, l3, bqk, bv, gate, lr, sb, mf, mb, up_pad, gbias_pad, gain, wout, x2d, lng, lnb)


def _layer_ab(x2d, batch, seq, w_in, gate_up, gate_bias, norm_gain, w_out, bias_tbls, lng, lnb, consts):
    mf, mb, msel = consts
    sizes = (3 * A_WIDTH + 2 * B_K_WIDTH + B_V_WIDTH, 2 * B_GATE_RANK, MIX_WIDTH)
    o0, o1 = sizes[0], sizes[0] + sizes[1]
    w_main = jnp.concatenate([w_in[:, :o0], w_in[:, o1:]], axis=1).astype(BF16)
    w_lr = jnp.pad(w_in[:, o0:o1], ((0, 0), (0, LANES - sizes[1]))).astype(BF16)
    up = gate_up.astype(F32)
    up_pad = jnp.zeros((LANES, 2 * B_K_WIDTH), F32)
    up_pad = up_pad.at[0:B_GATE_RANK, 0:B_K_WIDTH].set(up[0])
    up_pad = up_pad.at[B_GATE_RANK:2 * B_GATE_RANK, B_K_WIDTH:].set(up[1])
    gbias_pad = gate_bias.astype(F32).reshape(1, 2 * B_K_WIDTH)

    *a_splits, bqk, bv, gate, lr = _inproj_ab(x2d, w_main, w_lr, batch, seq)
    attn = [_attention_pattern(a, tbl, d) for a, tbl, d in zip(a_splits, bias_tbls, DILATIONS)]
    sb = _states_b(lr, bqk, bv, up_pad, gbias_pad, msel, batch, seq)
    gain = norm_gain.astype(F32).reshape(B_V_WIDTH // LANES, 1, LANES)
    wout = w_out.astype(BF16)
    return _output_ab(attn, bqk, bv, gate, lr, sb, mf, mb, up_pad, gbias_pad, gain, wout, x2d, lng, lnb,
                      batch, seq)


def _layer_c(x2d, batch, seq, w_in, lower_bounds, layer_idx, norm_gain, w_out, lng, lnb, consts):
    mf, mb, msel = consts
    cq, ci_, gate, lg = _inproj_c(x2d, w_in.astype(BF16), lower_bounds.astype(F32), layer_idx)
    sb = _states_c(lg, ci_, msel, batch, seq)
    gain = norm_gain.astype(F32).reshape(C_HEADS, 1, LANES)
    wout = w_out.astype(BF16)
    return _output_c(cq, ci_, gate, lg, sb, mf, mb, gain, wout, x2d, lng, lnb, batch, seq)


def kernel(x, w_in_ab, gla_gate_up, gla_gate_bias, gla_norm, w_out_ab, w_in_c, hgrn_lower_bounds, hgrn_norm,
           w_out_c, rel_bias, ln_gain, ln_bias):
    batch, seq, _ = x.shape
    assert seq % ATT_TOKENS == 0 and seq % ROW_BLOCK == 0 and (batch * seq) % PROJ_ROWS == 0
    consts = _scan_constants()
    bias_tbls = [_bias_tables(rel_bias, d) for d in DILATIONS]
    x2d = x.astype(F32).reshape(batch * seq, D_MODEL)
    for layer in range(DEPTH):
        lng = ln_gain[layer].astype(F32).reshape(1, D_MODEL)
        lnb = ln_bias[layer].astype(F32).reshape(1, D_MODEL)
        if layer % 2 == 0:
            e = layer // 2
            x2d = _layer_ab(x2d, batch, seq, w_in_ab[e], gla_gate_up[e], gla_gate_bias[e], gla_norm[e],
                            w_out_ab[e], bias_tbls, lng, lnb, consts)
        else:
            o = layer // 2
            x2d = _layer_c(x2d, batch, seq, w_in_c[o], hgrn_lower_bounds, layer, hgrn_norm[o], w_out_c[o],
                           lng, lnb, consts)
    return x2d.reshape(batch, seq, D_MODEL).astype(x.dtype)
```

```python
import functools

import numpy as np
import jax
import jax.numpy as jnp
from jax import lax
from jax.experimental import pallas as pl
from jax.experimental.pallas import tpu as pltpu

F32 = jnp.float32
BF16 = jnp.bfloat16

LANES = 128
VMEM_LIMIT = 56 * 1024 * 1024

D_MODEL = 1024
DEPTH = 4
A_HEADS = 8
A_HEAD_DIM = 64
A_WIDTH = A_HEADS * A_HEAD_DIM
A_GROUPS = A_WIDTH // LANES
DILATIONS = (1, 4, 16)
A_HALF_STEPS = 64
REL_BUCKETS = 32
REL_MAX_DISTANCE = 1024
B_HEADS = 4
B_KEY_DIM = 128
B_VAL_DIM = 256
B_K_WIDTH = B_HEADS * B_KEY_DIM
B_V_WIDTH = B_HEADS * B_VAL_DIM
B_GATE_RANK = 16
B_GATE_NORMALIZER = 16.0
C_HEADS = 12
C_HEAD_DIM = 128
C_WIDTH = C_HEADS * C_HEAD_DIM
MIX_WIDTH = A_WIDTH + B_V_WIDTH
MIX_GROUPS = MIX_WIDTH // LANES
NORM_EPS = 1e-5
DEEPNORM_ALPHA = (2 * DEPTH) ** 0.25

NEG = -1e30
LOG2E = 1.4426950408889634
ATT_Q_SCALE = A_HEAD_DIM ** -0.5 * LOG2E

CHUNK = 64
SUB = 32
ATT_Q = 128
ATT_K = ATT_Q + 2 * A_HALF_STEPS
ATT_TOKENS = 2048
ATT_BATCH = 8
ROW_BLOCK = 256
TOTAL_ROWS = 8
HEADS_PER_STEP = 12
PROJ_ROWS = 512


def _params(sem):
    return pltpu.CompilerParams(dimension_semantics=sem, vmem_limit_bytes=VMEM_LIMIT)


def _dot(a, b):
    return jnp.dot(a, b, preferred_element_type=F32)


def _dot_nt(a, b):
    return lax.dot_general(a, b, (((1,), (1,)), ((), ())), preferred_element_type=F32)


def _dot_tn(a, b):
    return lax.dot_general(a, b, (((0,), (0,)), ((), ())), preferred_element_type=F32)


def _split2(x):
    hi = x.astype(BF16)
    lo = (x - hi.astype(F32)).astype(BF16)
    return hi, lo


def _dot_sel(m01, x):
    n = x.shape[1]
    hi, lo = _split2(x)
    both = _dot(m01, jnp.concatenate([hi, lo], axis=1))
    return both[:, :n] + both[:, n:]


def _dot_f32(a, b):
    ah, al = _split2(a)
    bh, bl = _split2(b)
    return _dot(ah, bh) + _dot(ah, bl) + _dot(al, bh)


def _sigmoid(z):
    return 1.0 / (1.0 + jnp.exp(-z))


def _silu(z):
    return z * _sigmoid(z)


def _log_sigmoid(z):
    return jnp.minimum(z, 0.0) - jnp.log(1.0 + jnp.exp(-jnp.abs(z)))


def _layer_norm(z, g, b):
    mu = jnp.mean(z, axis=-1, keepdims=True)
    zc = z - mu
    var = jnp.mean(zc * zc, axis=-1, keepdims=True)
    return zc * lax.rsqrt(var + NORM_EPS) * g + b


def _store_groups(acc, out_ref, g0):
    for k in range(acc.shape[1] // LANES):
        out_ref[g0 + k] = acc[:, k * LANES:(k + 1) * LANES].astype(out_ref.dtype)


def _inproj_ab_kernel(x_ref, w_ref, wlr_ref, a1_ref, a4_ref, a16_ref, bqk_ref, bv_ref, gate_ref, lr_ref,
                      nat_scr, split_scr):
    xb = x_ref[...].astype(BF16)
    step = 4 * LANES
    tm = x_ref.shape[0]
    ratio = DILATIONS[1]
    for c in range(0, 3 * A_WIDTH, step):
        acc = _dot(xb, w_ref[:, c:c + step])
        if c == 0:
            acc = acc * ATT_Q_SCALE
        g0 = c // LANES
        for k in range(A_GROUPS):
            slab = acc[:, k * LANES:(k + 1) * LANES]
            nat_scr[k] = slab
            a1_ref[g0 + k, 0] = slab.astype(BF16)
        for k in range(A_GROUPS):
            for r in range(ratio):
                part = nat_scr[k, pl.ds(r, tm // ratio, stride=ratio), :]
                split_scr[k, r] = part
                a4_ref[g0 + k, r] = part.astype(BF16)
            for r in range(ratio):
                for r2 in range(ratio):
                    part = split_scr[k, r, pl.ds(r2, tm // ratio // ratio, stride=ratio), :]
                    a16_ref[g0 + k, r + ratio * r2] = part.astype(BF16)
    base = 3 * A_WIDTH
    for c in range(0, 2 * B_K_WIDTH, step):
        acc = _dot(xb, w_ref[:, base + c:base + c + step])
        if c < B_K_WIDTH:
            acc = acc * (B_KEY_DIM ** -0.5)
        _store_groups(acc, bqk_ref, c // LANES)
    base += 2 * B_K_WIDTH
    for c in range(0, B_V_WIDTH, step):
        _store_groups(_dot(xb, w_ref[:, base + c:base + c + step]), bv_ref, c // LANES)
    base += B_V_WIDTH
    for c in range(0, MIX_WIDTH, step):
        _store_groups(_dot(xb, w_ref[:, base + c:base + c + step]), gate_ref, c // LANES)
    lr_ref[...] = _dot(xb, wlr_ref[...])


def _inproj_ab(x2d, w_main, w_lr, batch, seq):
    T = x2d.shape[0]
    tm = PROJ_ROWS
    n_main = w_main.shape[1]
    nbs = seq // tm
    grp = lambda n: pl.BlockSpec((n, tm, LANES), lambda i: (0, i, 0))
    split = lambda d: pl.BlockSpec((3 * A_GROUPS, None, d, tm // d, LANES),
                                   lambda i: (0, i // nbs, 0, i % nbs, 0))
    split_shape = lambda d: jax.ShapeDtypeStruct((3 * A_GROUPS, batch, d, seq // d, LANES), BF16)
    return pl.pallas_call(
        _inproj_ab_kernel,
        grid=(T // tm,),
        in_specs=[
            pl.BlockSpec((tm, D_MODEL), lambda i: (i, 0)),
            pl.BlockSpec((D_MODEL, n_main), lambda i: (0, 0)),
            pl.BlockSpec((D_MODEL, LANES), lambda i: (0, 0)),
        ],
        out_specs=[split(d) for d in DILATIONS] + [grp(8), grp(8), grp(MIX_GROUPS),
                                                   pl.BlockSpec((tm, LANES), lambda i: (i, 0))],
        scratch_shapes=[pltpu.VMEM((A_GROUPS, tm, LANES), F32),
                        pltpu.VMEM((A_GROUPS, DILATIONS[1], tm // DILATIONS[1], LANES), F32)],
        out_shape=[split_shape(d) for d in DILATIONS] + [
            jax.ShapeDtypeStruct((8, T, LANES), BF16),
            jax.ShapeDtypeStruct((8, T, LANES), BF16),
            jax.ShapeDtypeStruct((MIX_GROUPS, T, LANES), BF16),
            jax.ShapeDtypeStruct((T, LANES), F32),
        ],
        compiler_params=_params(("parallel",)),
        name="inproj_ab",
    )(x2d, w_main, w_lr)


def _inproj_c_kernel(x_ref, w_ref, lbnd_ref, q_ref, i_ref, gate_ref, lg_ref, *, layer_idx):
    xb = x_ref[...].astype(BF16)
    lbnd = lbnd_ref[...]
    e = jnp.exp(lbnd - jnp.max(lbnd, axis=0, keepdims=True))
    sm = e / jnp.sum(e, axis=0, keepdims=True)
    lb = jnp.sum(sm[1:layer_idx + 1], axis=0, keepdims=True)
    step = 4 * LANES
    for c in range(0, C_WIDTH, step):
        q = _dot(xb, w_ref[:, c:c + step])
        _store_groups(_silu(q) * (C_HEAD_DIM ** -0.5), q_ref, c // LANES)
    for d in range(2):
        base = (1 + d) * C_WIDTH
        for c in range(0, C_WIDTH, step):
            z = _dot(xb, w_ref[:, base + c:base + c + step])
            lbc = lb[:, c:c + step]
            f = lbc + (1.0 - lbc) * _sigmoid(z)
            _store_groups(jnp.log2(f), lg_ref, (d * C_WIDTH + c) // LANES)
    for c in range(0, C_WIDTH, step):
        _store_groups(_dot(xb, w_ref[:, 3 * C_WIDTH + c:3 * C_WIDTH + c + step]), i_ref, c // LANES)
    for c in range(0, C_WIDTH, step):
        _store_groups(_dot(xb, w_ref[:, 4 * C_WIDTH + c:4 * C_WIDTH + c + step]), gate_ref, c // LANES)


def _inproj_c(x2d, w, lower_bounds, layer_idx):
    T = x2d.shape[0]
    tm = PROJ_ROWS // 2
    grp = lambda n: pl.BlockSpec((n, tm, LANES), lambda i: (0, i, 0))
    return pl.pallas_call(
        functools.partial(_inproj_c_kernel, layer_idx=layer_idx),
        grid=(T // tm,),
        in_specs=[
            pl.BlockSpec((tm, D_MODEL), lambda i: (i, 0)),
            pl.BlockSpec((D_MODEL, 5 * C_WIDTH), lambda i: (0, 0)),
            pl.BlockSpec((DEPTH, C_WIDTH), lambda i: (0, 0)),
        ],
        out_specs=[grp(C_HEADS), grp(C_HEADS), grp(C_HEADS), grp(2 * C_HEADS)],
        out_shape=[
            jax.ShapeDtypeStruct((C_HEADS, T, LANES), BF16),
            jax.ShapeDtypeStruct((C_HEADS, T, LANES), BF16),
            jax.ShapeDtypeStruct((C_HEADS, T, LANES), BF16),
            jax.ShapeDtypeStruct((2 * C_HEADS, T, LANES), F32),
        ],
        compiler_params=_params(("parallel",)),
        name="inproj_c",
    )(x2d, w, lower_bounds)


def _t5_bucket(rel):
    half = REL_BUCKETS // 2
    max_exact = half // 2
    n = np.abs(rel)
    large = max_exact + (np.log(np.maximum(n, 1) / max_exact)
                         / np.log(REL_MAX_DISTANCE / max_exact) * (half - max_exact)).astype(np.int32)
    large = np.minimum(large, half - 1)
    return np.where(rel > 0, half, 0) + np.where(n < max_exact, n, large)


def _bias_tables(rel_bias, dilation):
    offs = np.arange(-A_HALF_STEPS, A_HALF_STEPS + 1)
    vals = rel_bias.astype(F32)[_t5_bucket(offs * dilation)] * LOG2E
    period = ATT_K + ATT_Q
    ext = jnp.concatenate([vals, jnp.full((period - vals.shape[0], A_HEADS), NEG, F32)], axis=0).T
    flat = jnp.broadcast_to(ext[:, None, :], (A_HEADS, ATT_Q, period)).reshape(A_HEADS, ATT_Q * period)
    tbl = flat[:, :ATT_Q * (period - 1)].reshape(A_HEADS, ATT_Q, period - 1)[:, :, :ATT_K]
    col = np.arange(ATT_K)
    before, after = col < A_HALF_STEPS, col >= ATT_K - A_HALF_STEPS
    variants = []
    for mask in (np.zeros_like(before), before, after, before | after):
        t = jnp.where(mask[None, None, :], NEG, tbl).reshape(A_GROUPS, 2, ATT_Q, ATT_K)
        variants.append(jnp.transpose(t, (0, 2, 1, 3)).reshape(A_GROUPS, ATT_Q, 2 * ATT_K))
    return jnp.stack(variants, axis=0)


def _attn_kernel(q_ref, kp_ref, kc_ref, kn_ref, vp_ref, vc_ref, vn_ref, bias_ref, o_ref, lse_ref, *,
                 dilation, rows):
    j = pl.program_id(1)
    first = (j == 0).astype(jnp.int32)
    last = (j == pl.num_programs(1) - 1).astype(jnp.int32)
    halo = A_HALF_STEPS
    n_sub = rows // ATT_Q
    low_kv = lax.broadcasted_iota(jnp.int32, (ATT_K, LANES), 1) < A_HEAD_DIM
    low_q = lax.broadcasted_iota(jnp.int32, (ATT_Q, LANES), 1) < A_HEAD_DIM
    zero = jnp.zeros((ATT_K, LANES), BF16)
    one = jnp.ones((ATT_K, LANES), BF16)

    def window(prev_ref, cur_ref, next_ref, g, r, i):
        lo, hi = i * ATT_Q - halo, i * ATT_Q - halo + ATT_K
        parts = [prev_ref[g, r]] if lo < 0 else []
        parts.append(cur_ref[g, r, max(lo, 0):min(hi, rows), :])
        if hi > rows:
            parts.append(next_ref[g, r])
        return parts[0] if len(parts) == 1 else jnp.concatenate(parts, axis=0)

    units = [(r, i) for r in range(dilation) for i in range(n_sub)]

    def scores(g, r, i):
        variant = (first if i == 0 else 0) + (2 * last if i == n_sub - 1 else 0)
        bias = bias_ref[variant, g]
        q2 = q_ref[g, r, i * ATT_Q:(i + 1) * ATT_Q, :]
        k2 = window(kp_ref, kc_ref, kn_ref, g, r, i)
        s0 = _dot_nt(q2, jnp.where(low_kv, k2, zero)) + bias[:, :ATT_K]
        s1 = _dot_nt(q2, jnp.where(low_kv, zero, k2)) + bias[:, ATT_K:]
        return s0, s1

    def softmax(s):
        m = jnp.max(s, axis=1, keepdims=True)
        return m, jnp.exp2(s - m).astype(BF16)

    def finish(g, r, i, m0, p0, m1, p1):
        v2 = window(vp_ref, vc_ref, vn_ref, g, r, i)
        r0 = _dot(p0, jnp.where(low_kv, v2, one))
        r1 = _dot(p1, jnp.where(low_kv, one, v2))
        pv = jnp.where(low_q, r0, r1)
        denom = pltpu.roll(jnp.where(low_q, r1, r0), A_HEAD_DIM, axis=1)
        m = jnp.where(low_q, m0, m1)
        start = r + dilation * i * ATT_Q
        dst = pl.ds(start, ATT_Q) if dilation == 1 else pl.ds(start, ATT_Q, stride=dilation)
        o_ref[g, dst, :] = pv / denom
        lse_ref[g, dst, :] = m + jnp.log2(denom)

    def group_body(g, carry):
        for u0 in range(0, len(units), ATT_BATCH):
            batch = units[u0:u0 + ATT_BATCH]
            s = [scores(g, r, i) for r, i in batch]
            sm = [(softmax(s0), softmax(s1)) for s0, s1 in s]
            for (r, i), ((m0, p0), (m1, p1)) in zip(batch, sm):
                finish(g, r, i, m0, p0, m1, p1)
        return carry

    lax.fori_loop(0, A_GROUPS, group_body, 0)


def _attention_pattern(a_split, bias_tbl, dilation):
    _, batch, _, L, _ = a_split.shape
    T = batch * L * dilation
    rows = ATT_TOKENS // dilation
    halo = A_HALF_STEPS
    hb = rows // halo
    nj = L // rows
    cur = lambda part: pl.BlockSpec((A_GROUPS, None, dilation, rows, LANES), lambda b, j: (part, b, 0, j, 0))
    prev = lambda part: pl.BlockSpec((A_GROUPS, None, dilation, halo, LANES),
                                     lambda b, j: (part, b, 0, jnp.maximum(j * hb - 1, 0), 0))
    nxt = lambda part: pl.BlockSpec((A_GROUPS, None, dilation, halo, LANES),
                                    lambda b, j: (part, b, 0, jnp.minimum((j + 1) * hb, L // halo - 1), 0))
    out_spec = pl.BlockSpec((A_GROUPS, ATT_TOKENS, LANES), lambda b, j: (0, b * nj + j, 0))
    return pl.pallas_call(
        functools.partial(_attn_kernel, dilation=dilation, rows=rows),
        grid=(batch, nj),
        in_specs=[cur(0), prev(1), cur(1), nxt(1), prev(2), cur(2), nxt(2),
                  pl.BlockSpec(bias_tbl.shape, lambda b, j: (0, 0, 0, 0))],
        out_specs=[out_spec, out_spec],
        out_shape=[jax.ShapeDtypeStruct((A_GROUPS, T, LANES), F32)] * 2,
        compiler_params=_params(("parallel", "parallel")),
        name=f"dilated_attn_d{dilation}",
    )(a_split, a_split, a_split, a_split, a_split, a_split, a_split, bias_tbl)


def _scan_constants():
    t = np.arange(ROW_BLOCK)[:, None]
    r = np.arange(ROW_BLOCK)[None, :]
    same = t // CHUNK == r // CHUNK
    totals = np.repeat(np.arange(ROW_BLOCK // CHUNK), TOTAL_ROWS)[:, None] == r // CHUNK
    prefix = same & (r <= t)
    suffix = same & (r >= t)
    state_b = np.concatenate([same & (r < t), totals], axis=0)
    as_bf16 = lambda m: jnp.asarray(m.astype(np.float32), BF16)
    return as_bf16(prefix), as_bf16(suffix), as_bf16(state_b)


def _intra_masks():
    t = lax.broadcasted_iota(jnp.int32, (CHUNK, CHUNK), 0)
    s = lax.broadcasted_iota(jnp.int32, (CHUNK, CHUNK), 1)
    tb, sb = t // SUB, s // SUB
    return (tb == sb) & (s <= t), tb > sb, (tb == sb) & (s > t)


N_OPERANDS = 6


def _decayed_operands(cum_ref, q_rows, k_rows, op_ref, backward):
    per_chunk = CHUNK // SUB
    half = SUB // 2
    for b in range(ROW_BLOCK // SUB):
        r0 = b * SUB
        rows = slice(r0, r0 + SUB)
        row = lambda i: cum_ref[i:i + 1, :]
        if backward:
            mid, r_out = row(r0 + half), row(r0)
            r_in = None if b % per_chunk == per_chunk - 1 else row(r0 + SUB)
        else:
            mid, r_out = row(r0 + half - 1), row(r0 + SUB - 1)
            r_in = None if b % per_chunk == 0 else row(r0 - 1)
        p = cum_ref[rows, :]
        q, k = q_rows(rows), k_rows(rows)
        d_mid = p - mid
        q_in = q * jnp.exp2(p if r_in is None else p - r_in)
        q_dec = q_in if r_in is None else q_in * jnp.exp2(r_in)
        k_out = k * jnp.exp2(r_out - p)
        vals = [q * jnp.exp2(d_mid), k * jnp.exp2(-d_mid), q_in, k_out, q_dec]
        if not backward:
            chunk_end = row((b // per_chunk + 1) * CHUNK - 1)
            vals.append(k_out if b % per_chunk == per_chunk - 1 else k_out * jnp.exp2(chunk_end - r_out))
        for i, val in enumerate(vals):
            op_ref[i, rows, :] = val.astype(BF16)


def _intra_matrix(ci, opf, opb, masks):
    m_df, m_of, m_db = masks
    rows = slice(ci * CHUNK, (ci + 1) * CHUNK)
    a = jnp.where(m_df, _dot_nt(opf[0, rows, :], opf[1, rows, :]),
                  jnp.where(m_of, _dot_nt(opf[2, rows, :], opf[3, rows, :]),
                            jnp.where(m_db, _dot_nt(opb[0, rows, :], opb[1, rows, :]),
                                      _dot_nt(opb[2, rows, :], opb[3, rows, :]))))
    return a.astype(BF16)


def _chunk_output(ci, a, opf, opb, v, stf, stb):
    rows = slice(ci * CHUNK, (ci + 1) * CHUNK)
    q_dec = jnp.concatenate([opf[4, rows, :], opb[4, rows, :]], axis=1)
    return _dot(a, v) + _dot_nt(q_dec, jnp.concatenate([stf, stb], axis=1))


def _state_pass(n_heads, nv, get_lg, get_k, v_ref, msel, out_ref, st_ref, kdec_scr, etot_scr):
    order = range(ROW_BLOCK // CHUNK - 1, -1, -1)
    for h in range(n_heads):
        lg = get_lg(h)
        d = _dot_sel(msel, lg)
        kdec_scr[h] = (get_k(h, lg) * jnp.exp2(d[0:ROW_BLOCK])).astype(BF16)
        etot_scr[h] = jnp.exp2(d[ROW_BLOCK:])
    for h in range(n_heads):
        ds = {}
        for ci in order:
            rows = slice(ci * CHUNK, (ci + 1) * CHUNK)
            ds[ci] = _dot_tn(_head_value(v_ref, h, rows, nv), kdec_scr[h, rows, :])
        st = st_ref[h]
        for ci in order:
            out_ref[ci, h] = st.astype(out_ref.dtype)
            st = st * etot_scr[h, ci * TOTAL_ROWS:ci * TOTAL_ROWS + 1, :] + ds[ci]
        st_ref[h] = st


def _head_value(v_ref, h, rows, nv):
    if nv == 1:
        return v_ref[h, rows, :]
    return jnp.concatenate([v_ref[nv * h + k, rows, :] for k in range(nv)], axis=1)


def _gla_log_gates(lr, up_ref, bias_ref, lg_scr, directions):
    for d in directions:
        cols = slice(d * B_K_WIDTH, (d + 1) * B_K_WIDTH)
        z = _dot_f32(lr, up_ref[:, cols]) + bias_ref[:, cols]
        lg = _log_sigmoid(z) * (LOG2E / B_GATE_NORMALIZER)
        for h in range(B_HEADS):
            lg_scr[d * B_HEADS + h] = lg[:, h * LANES:(h + 1) * LANES]


def _state_kernel_c(lg_ref, v_ref, msel_ref, out_ref, st_ref, kdec_scr, etot_scr):
    @pl.when(pl.program_id(1) == 0)
    def _():
        st_ref[...] = jnp.zeros_like(st_ref)

    _state_pass(C_HEADS, 1, lambda h: lg_ref[h], lambda h, lg: 1.0 - jnp.exp2(lg), v_ref, msel_ref[...],
                out_ref, st_ref, kdec_scr, etot_scr)


def _state_kernel_b(lr_ref, k_ref, v_ref, up_ref, bias_ref, msel_ref, out_ref, st_ref, lg_scr, kdec_scr,
                    etot_scr):
    @pl.when(pl.program_id(1) == 0)
    def _():
        st_ref[...] = jnp.zeros_like(st_ref)

    _gla_log_gates(lr_ref[...], up_ref, bias_ref, lg_scr, directions=(1,))
    _state_pass(B_HEADS, B_VAL_DIM // LANES, lambda h: lg_scr[B_HEADS + h], lambda h, lg: k_ref[h].astype(F32),
                v_ref, msel_ref[...], out_ref, st_ref, kdec_scr, etot_scr)


def _reversed_block_index(blocks_per_seq):
    return lambda b, n: b * blocks_per_seq + (blocks_per_seq - 1 - n)


def _states_c(lg, v, msel, batch, seq):
    T = v.shape[1]
    nb = seq // ROW_BLOCK
    cpb = ROW_BLOCK // CHUNK
    blk = _reversed_block_index(nb)
    return pl.pallas_call(
        _state_kernel_c,
        grid=(batch, nb),
        in_specs=[
            pl.BlockSpec((C_HEADS, ROW_BLOCK, LANES), lambda b, n: (1, blk(b, n), 0)),
            pl.BlockSpec((C_HEADS, ROW_BLOCK, LANES), lambda b, n: (0, blk(b, n), 0)),
            pl.BlockSpec(msel.shape, lambda b, n: (0, 0)),
        ],
        out_specs=pl.BlockSpec((cpb, C_HEADS, C_HEAD_DIM, C_HEAD_DIM), lambda b, n: (blk(b, n), 0, 0, 0)),
        out_shape=jax.ShapeDtypeStruct((T // CHUNK, C_HEADS, C_HEAD_DIM, C_HEAD_DIM), BF16),
        scratch_shapes=[pltpu.VMEM((C_HEADS, C_HEAD_DIM, C_HEAD_DIM), F32),
                        pltpu.VMEM((C_HEADS, ROW_BLOCK, LANES), BF16),
                        pltpu.VMEM((C_HEADS, cpb * TOTAL_ROWS, LANES), F32)],
        compiler_params=_params(("parallel", "arbitrary")),
        name="hgrn_states_bwd",
    )(lg, v, msel)


def _states_b(lr, bqk, bv, up_pad, bias_pad, msel, batch, seq):
    T = lr.shape[0]
    nb = seq // ROW_BLOCK
    cpb = ROW_BLOCK // CHUNK
    blk = _reversed_block_index(nb)
    full = lambda a: pl.BlockSpec(a.shape, lambda b, n: (0,) * a.ndim)
    return pl.pallas_call(
        _state_kernel_b,
        grid=(batch, nb),
        in_specs=[
            pl.BlockSpec((ROW_BLOCK, LANES), lambda b, n: (blk(b, n), 0)),
            pl.BlockSpec((B_HEADS, ROW_BLOCK, LANES), lambda b, n: (1, blk(b, n), 0)),
            pl.BlockSpec((2 * B_HEADS, ROW_BLOCK, LANES), lambda b, n: (0, blk(b, n), 0)),
            full(up_pad), full(bias_pad), full(msel),
        ],
        out_specs=pl.BlockSpec((cpb, B_HEADS, B_VAL_DIM, B_KEY_DIM), lambda b, n: (blk(b, n), 0, 0, 0)),
        out_shape=jax.ShapeDtypeStruct((T // CHUNK, B_HEADS, B_VAL_DIM, B_KEY_DIM), BF16),
        scratch_shapes=[pltpu.VMEM((B_HEADS, B_VAL_DIM, B_KEY_DIM), F32),
                        pltpu.VMEM((2 * B_HEADS, ROW_BLOCK, LANES), F32),
                        pltpu.VMEM((B_HEADS, ROW_BLOCK, LANES), BF16),
                        pltpu.VMEM((B_HEADS, cpb * TOTAL_ROWS, LANES), F32)],
        compiler_params=_params(("parallel", "arbitrary")),
        name="gla_states_bwd",
    )(lr, bqk, bv, up_pad, bias_pad, msel)


def _project_and_norm(y_scr, wout_ref, x_ref, lng_ref, lnb_ref, out_ref):
    y = jnp.concatenate([y_scr[g] for g in range(MIX_GROUPS)], axis=1)
    z = DEEPNORM_ALPHA * x_ref[...] + _dot(y, wout_ref[...])
    out_ref[...] = _layer_norm(z, lng_ref[...], lnb_ref[...])


def _rms_gain(o, gain):
    return o * lax.rsqrt(jnp.mean(o * o, axis=-1, keepdims=True) + NORM_EPS) * gain


def _scan_heads(n_heads, nv, lg_of, q_rows_of, k_rows_of, v_ref, st_ref, sb_ref, mpre, msuf, scan_scr, finish):
    masks = _intra_masks()

    cum_scr, op_scr, a_scr, ds_scr, stf_scr = scan_scr

    @pl.when(pl.program_id(1) == 0)
    def _():
        st_ref[...] = jnp.zeros_like(st_ref)

    per_step = cum_scr.shape[0]

    def step(i, carry):
        heads = [i * per_step + j for j in range(per_step)]
        for j, h in enumerate(heads):
            cum_scr[j, 0] = _dot_sel(mpre, lg_of(h, 0))
            cum_scr[j, 1] = _dot_sel(msuf, lg_of(h, 1))
        for j, h in enumerate(heads):
            for d in range(2):
                _decayed_operands(cum_scr.at[j, d], q_rows_of(h), k_rows_of(h, d), op_scr.at[j, d], d == 1)
        units = [(j, h, ci) for j, h in enumerate(heads) for ci in range(ROW_BLOCK // CHUNK)]
        chunk_rows = lambda ci: slice(ci * CHUNK, (ci + 1) * CHUNK)
        for j, h, ci in units:
            a_scr[j, ci] = _intra_matrix(ci, op_scr.at[j, 0], op_scr.at[j, 1], masks)
            ds_scr[j, ci] = _dot_tn(_head_value(v_ref, h, chunk_rows(ci), nv),
                                    op_scr[j, 0, N_OPERANDS - 1, chunk_rows(ci), :])
        for j, h in enumerate(heads):
            st = st_ref[h]
            for ci in range(ROW_BLOCK // CHUNK):
                stf_scr[j, ci] = st.astype(BF16)
                total = cum_scr[j, 0, (ci + 1) * CHUNK - 1:(ci + 1) * CHUNK, :]
                st = st * jnp.exp2(total) + ds_scr[j, ci]
            st_ref[h] = st
        for j, h, ci in units:
            o = _chunk_output(ci, a_scr[j, ci], op_scr.at[j, 0], op_scr.at[j, 1],
                              _head_value(v_ref, h, chunk_rows(ci), nv), stf_scr[j, ci], sb_ref[ci, h])
            finish(h, chunk_rows(ci), o)
        return carry

    lax.fori_loop(0, n_heads // per_step, step, 0)


def _out_kernel_c(q_ref, i_ref, gate_ref, lgf_ref, lgb_ref, sb_ref, mpre_ref, msuf_ref, gain_ref,
                  wout_ref, x_ref, lng_ref, lnb_ref, out_ref, y_scr, st_scr, *scan_scr):
    lg_refs = (lgf_ref, lgb_ref)

    def finish(h, rows, o):
        y = _rms_gain(o, gain_ref[h]) * _silu(gate_ref[h, rows, :].astype(F32))
        y_scr[h, rows, :] = y.astype(BF16)

    _scan_heads(C_HEADS, 1,
                lambda h, d: lg_refs[d][h],
                lambda h: lambda rows: q_ref[h, rows, :].astype(F32),
                lambda h, d: lambda rows: 1.0 - jnp.exp2(lg_refs[d][h, rows, :]),
                i_ref, st_scr, sb_ref, mpre_ref[...], msuf_ref[...], scan_scr, finish)
    _project_and_norm(y_scr, wout_ref, x_ref, lng_ref, lnb_ref, out_ref)


def _out_kernel_ab(o1_ref, o2_ref, o3_ref, l1_ref, l2_ref, l3_ref, qk_ref, v_ref, gate_ref, lr_ref,
                   sb_ref, mf_ref, mb_ref, up_ref, gbias_ref, gain_ref,
                   wout_ref, x_ref, lng_ref, lnb_ref, out_ref, y_scr, lg_scr, st_scr, *scan_scr):
    for g in range(A_GROUPS):
        l1, l2, l3 = l1_ref[g], l2_ref[g], l3_ref[g]
        m = jnp.maximum(jnp.maximum(l1, l2), l3)
        e1, e2, e3 = jnp.exp2(l1 - m), jnp.exp2(l2 - m), jnp.exp2(l3 - m)
        num = e1 * o1_ref[g].astype(F32) + e2 * o2_ref[g].astype(F32) + e3 * o3_ref[g].astype(F32)
        oa = num / (e1 + e2 + e3)
        y_scr[g] = (oa * _silu(gate_ref[g].astype(F32))).astype(BF16)

    _gla_log_gates(lr_ref[...], up_ref, gbias_ref, lg_scr, directions=(0, 1))
    nv = B_VAL_DIM // LANES

    def finish(h, rows, o):
        gain = jnp.concatenate([gain_ref[nv * h + k] for k in range(nv)], axis=1)
        o = _rms_gain(o, gain)
        for kk in range(nv):
            grp = A_GROUPS + nv * h + kk
            y = o[:, kk * LANES:(kk + 1) * LANES] * _silu(gate_ref[grp, rows, :].astype(F32))
            y_scr[grp, rows, :] = y.astype(BF16)

    _scan_heads(B_HEADS, nv,
                lambda h, d: lg_scr[d * B_HEADS + h],
                lambda h: lambda rows: qk_ref[h, rows, :].astype(F32),
                lambda h, d: lambda rows: qk_ref[B_HEADS + h, rows, :].astype(F32),
                v_ref, st_scr, sb_ref, mf_ref[...], mb_ref[...], scan_scr, finish)
    _project_and_norm(y_scr, wout_ref, x_ref, lng_ref, lnb_ref, out_ref)


def _scan_scratch(tb, n_heads, v_dim, k_dim):
    cpb = tb // CHUNK
    per_step = min(HEADS_PER_STEP, n_heads)
    assert n_heads % per_step == 0
    return [pltpu.VMEM((per_step, 2, tb, LANES), F32),
            pltpu.VMEM((per_step, 2, N_OPERANDS, tb, LANES), BF16),
            pltpu.VMEM((per_step, cpb, CHUNK, CHUNK), BF16),
            pltpu.VMEM((per_step, cpb, v_dim, k_dim), F32),
            pltpu.VMEM((per_step, cpb, v_dim, k_dim), BF16)]


def _row_specs(tb, nb):
    grp = lambda n, part=0: pl.BlockSpec((n, tb, LANES), lambda b, i: (part, b * nb + i, 0))
    full = lambda a: pl.BlockSpec(a.shape, lambda b, i: (0,) * a.ndim)
    rows = lambda width: pl.BlockSpec((tb, width), lambda b, i: (b * nb + i, 0))
    return grp, full, rows


def _output_c(cq, ci_, gate, lg, sb, mf, mb, gain, wout, x2d, lng, lnb, batch, seq):
    T = x2d.shape[0]
    tb = ROW_BLOCK
    nb = seq // tb
    cpb = tb // CHUNK
    grp, full, rows = _row_specs(tb, nb)
    st = pl.BlockSpec((cpb, C_HEADS, C_HEAD_DIM, C_HEAD_DIM), lambda b, i: (b * nb + i, 0, 0, 0))
    return pl.pallas_call(
        _out_kernel_c,
        grid=(batch, nb),
        in_specs=[grp(C_HEADS), grp(C_HEADS), grp(C_HEADS), grp(C_HEADS, 0), grp(C_HEADS, 1), st,
                  full(mf), full(mb), full(gain), full(wout), rows(D_MODEL), full(lng), full(lnb)],
        out_specs=rows(D_MODEL),
        out_shape=jax.ShapeDtypeStruct((T, D_MODEL), F32),
        scratch_shapes=[pltpu.VMEM((MIX_GROUPS, tb, LANES), BF16),
                        pltpu.VMEM((C_HEADS, C_HEAD_DIM, C_HEAD_DIM), F32)]
        + _scan_scratch(tb, C_HEADS, C_HEAD_DIM, C_HEAD_DIM),
        compiler_params=_params(("parallel", "arbitrary")),
        name="hgrn_output",
    )(cq, ci_, gate, lg, lg, sb, mf, mb, gain, wout, x2d, lng, lnb)


def _output_ab(attn, bqk, bv, gate, lr, sb, mf, mb, up_pad, gbias_pad, gain, wout, x2d, lng, lnb, batch, seq):
    T = x2d.shape[0]
    tb = ROW_BLOCK
    nb = seq // tb
    cpb = tb // CHUNK
    grp, full, rows = _row_specs(tb, nb)
    st = pl.BlockSpec((cpb, B_HEADS, B_VAL_DIM, B_KEY_DIM), lambda b, i: (b * nb + i, 0, 0, 0))
    (o1, l1), (o2, l2), (o3, l3) = attn
    return pl.pallas_call(
        _out_kernel_ab,
        grid=(batch, nb),
        in_specs=[grp(A_GROUPS)] * 6 + [grp(2 * B_HEADS), grp(8), grp(MIX_GROUPS), rows(LANES), st,
                                        full(mf), full(mb), full(up_pad), full(gbias_pad), full(gain),
                                        full(wout), rows(D_MODEL), full(lng), full(lnb)],
        out_specs=rows(D_MODEL),
        out_shape=jax.ShapeDtypeStruct((T, D_MODEL), F32),
        scratch_shapes=[pltpu.VMEM((MIX_GROUPS, tb, LANES), BF16),
                        pltpu.VMEM((2 * B_HEADS, tb, LANES), F32),
                        pltpu.VMEM((B_HEADS, B_VAL_DIM, B_KEY_DIM), F32)]
        + _scan_scratch(tb, B_HEADS, B_VAL_DIM, B_KEY_DIM),
        compiler_params=_params(("parallel", "arbitrary")),
        name="ab_output",
    )(o1, o2, o3, l1, l2, l3, bqk, bv, gate, lr, sb, mf, mb, up_pad, gbias_pad, gain, wout, x2d, lng, lnb)


def _layer_ab(x2d, batch, seq, w_in, gate_up, gate_bias, norm_gain, w_out, bias_tbls, lng, lnb, consts):
    mf, mb, msel = consts
    sizes = (3 * A_WIDTH + 2 * B_K_WIDTH + B_V_WIDTH, 2 * B_GATE_RANK, MIX_WIDTH)
    o0, o1 = sizes[0], sizes[0] + sizes[1]
    w_main = jnp.concatenate([w_in[:, :o0], w_in[:, o1:]], axis=1).astype(BF16)
    w_lr = jnp.pad(w_in[:, o0:o1], ((0, 0), (0, LANES - sizes[1]))).astype(BF16)
    up = gate_up.astype(F32)
    up_pad = jnp.zeros((LANES, 2 * B_K_WIDTH), F32)
    up_pad = up_pad.at[0:B_GATE_RANK, 0:B_K_WIDTH].set(up[0])
    up_pad = up_pad.at[B_GATE_RANK:2 * B_GATE_RANK, B_K_WIDTH:].set(up[1])
    gbias_pad = gate_bias.astype(F32).reshape(1, 2 * B_K_WIDTH)

    *a_splits, bqk, bv, gate, lr = _inproj_ab(x2d, w_main, w_lr, batch, seq)
    attn = [_attention_pattern(a, tbl, d) for a, tbl, d in zip(a_splits, bias_tbls, DILATIONS)]
    sb = _states_b(lr, bqk, bv, up_pad, gbias_pad, msel, batch, seq)
    gain = norm_gain.astype(F32).reshape(B_V_WIDTH // LANES, 1, LANES)
    wout = w_out.astype(BF16)
    return _output_ab(attn, bqk, bv, gate, lr, sb, mf, mb, up_pad, gbias_pad, gain, wout, x2d, lng, lnb,
                      batch, seq)


def _layer_c(x2d, batch, seq, w_in, lower_bounds, layer_idx, norm_gain, w_out, lng, lnb, consts):
    mf, mb, msel = consts
    cq, ci_, gate, lg = _inproj_c(x2d, w_in.astype(BF16), lower_bounds.astype(F32), layer_idx)
    sb = _states_c(lg, ci_, msel, batch, seq)
    gain = norm_gain.astype(F32).reshape(C_HEADS, 1, LANES)
    wout = w_out.astype(BF16)
    return _output_c(cq, ci_, gate, lg, sb, mf, mb, gain, wout, x2d, lng, lnb, batch, seq)


def kernel(x, w_in_ab, gla_gate_up, gla_gate_bias, gla_norm, w_out_ab, w_in_c, hgrn_lower_bounds, hgrn_norm,
           w_out_c, rel_bias, ln_gain, ln_bias):
    batch, seq, _ = x.shape
    assert seq % ATT_TOKENS == 0 and seq % ROW_BLOCK == 0 and (batch * seq) % PROJ_ROWS == 0
    consts = _scan_constants()
    bias_tbls = [_bias_tables(rel_bias, d) for d in DILATIONS]
    x2d = x.astype(F32).reshape(batch * seq, D_MODEL)
    for layer in range(DEPTH):
        lng = ln_gain[layer].astype(F32).reshape(1, D_MODEL)
        lnb = ln_bias[layer].astype(F32).reshape(1, D_MODEL)
        if layer % 2 == 0:
            e = layer // 2
            x2d = _layer_ab(x2d, batch, seq, w_in_ab[e], gla_gate_up[e], gla_gate_bias[e], gla_norm[e],
                            w_out_ab[e], bias_tbls, lng, lnb, consts)
        else:
            o = layer // 2
            x2d = _layer_c(x2d, batch, seq, w_in_c[o], hgrn_lower_bounds, layer, hgrn_norm[o], w_out_c[o],
                           lng, lnb, consts)
    return x2d.reshape(batch, seq, D_MODEL).astype(x.dtype)
```

```python
import functools

import numpy as np
import jax
import jax.numpy as jnp
from jax import lax
from jax.experimental import pallas as pl
from jax.experimental.pallas import tpu as pltpu

F32 = jnp.float32
BF16 = jnp.bfloat16

LANES = 128
VMEM_LIMIT = 56 * 1024 * 1024

D_MODEL = 1024
DEPTH = 4
A_HEADS = 8
A_HEAD_DIM = 64
A_WIDTH = A_HEADS * A_HEAD_DIM
A_GROUPS = A_WIDTH // LANES
DILATIONS = (1, 4, 16)
A_HALF_STEPS = 64
REL_BUCKETS = 32
REL_MAX_DISTANCE = 1024
B_HEADS = 4
B_KEY_DIM = 128
B_VAL_DIM = 256
B_K_WIDTH = B_HEADS * B_KEY_DIM
B_V_WIDTH = B_HEADS * B_VAL_DIM
B_GATE_RANK = 16
B_GATE_NORMALIZER = 16.0
C_HEADS = 12
C_HEAD_DIM = 128
C_WIDTH = C_HEADS * C_HEAD_DIM
MIX_WIDTH = A_WIDTH + B_V_WIDTH
MIX_GROUPS = MIX_WIDTH // LANES
NORM_EPS = 1e-5
DEEPNORM_ALPHA = (2 * DEPTH) ** 0.25

NEG = -1e30
LOG2E = 1.4426950408889634
ATT_Q_SCALE = A_HEAD_DIM ** -0.5 * LOG2E

CHUNK = 64
SUB = 32
ATT_Q = 128
ATT_K = ATT_Q + 2 * A_HALF_STEPS
ATT_TOKENS = 2048
ATT_BATCH = 8
ROW_BLOCK = 256
STATE_BLOCK = 512
TOTAL_ROWS = 8
HEADS_PER_STEP = 12
PROJ_ROWS = 512


def _params(sem):
    return pltpu.CompilerParams(dimension_semantics=sem, vmem_limit_bytes=VMEM_LIMIT)


def _dot(a, b):
    return jnp.dot(a, b, preferred_element_type=F32)


def _dot_nt(a, b):
    return lax.dot_general(a, b, (((1,), (1,)), ((), ())), preferred_element_type=F32)


def _dot_tn(a, b):
    return lax.dot_general(a, b, (((0,), (0,)), ((), ())), preferred_element_type=F32)


def _split2(x):
    hi = x.astype(BF16)
    lo = (x - hi.astype(F32)).astype(BF16)
    return hi, lo


def _dot_sel(m01, x):
    n = x.shape[1]
    hi, lo = _split2(x)
    both = _dot(m01, jnp.concatenate([hi, lo], axis=1))
    return both[:, :n] + both[:, n:]


def _dot_f32(a, b):
    ah, al = _split2(a)
    bh, bl = _split2(b)
    return _dot(ah, bh) + _dot(ah, bl) + _dot(al, bh)


def _sigmoid(z):
    return 1.0 / (1.0 + jnp.exp(-z))


def _silu(z):
    return z * _sigmoid(z)


def _log_sigmoid(z):
    return jnp.minimum(z, 0.0) - jnp.log(1.0 + jnp.exp(-jnp.abs(z)))


def _layer_norm(z, g, b):
    mu = jnp.mean(z, axis=-1, keepdims=True)
    zc = z - mu
    var = jnp.mean(zc * zc, axis=-1, keepdims=True)
    return zc * lax.rsqrt(var + NORM_EPS) * g + b


def _store_groups(acc, out_ref, g0):
    for k in range(acc.shape[1] // LANES):
        out_ref[g0 + k] = acc[:, k * LANES:(k + 1) * LANES].astype(out_ref.dtype)


def _inproj_ab_kernel(x_ref, w_ref, wlr_ref, a1_ref, a4_ref, a16_ref, bqk_ref, bv_ref, gate_ref, lr_ref,
                      nat_scr, split_scr):
    xb = x_ref[...].astype(BF16)
    step = 4 * LANES
    tm = x_ref.shape[0]
    ratio = DILATIONS[1]
    for c in range(0, 3 * A_WIDTH, step):
        acc = _dot(xb, w_ref[:, c:c + step])
        if c == 0:
            acc = acc * ATT_Q_SCALE
        g0 = c // LANES
        for k in range(A_GROUPS):
            slab = acc[:, k * LANES:(k + 1) * LANES]
            nat_scr[k] = slab
            a1_ref[g0 + k, 0] = slab.astype(BF16)
        for k in range(A_GROUPS):
            for r in range(ratio):
                part = nat_scr[k, pl.ds(r, tm // ratio, stride=ratio), :]
                split_scr[k, r] = part
                a4_ref[g0 + k, r] = part.astype(BF16)
            for r in range(ratio):
                for r2 in range(ratio):
                    part = split_scr[k, r, pl.ds(r2, tm // ratio // ratio, stride=ratio), :]
                    a16_ref[g0 + k, r + ratio * r2] = part.astype(BF16)
    base = 3 * A_WIDTH
    for c in range(0, 2 * B_K_WIDTH, step):
        acc = _dot(xb, w_ref[:, base + c:base + c + step])
        if c < B_K_WIDTH:
            acc = acc * (B_KEY_DIM ** -0.5)
        _store_groups(acc, bqk_ref, c // LANES)
    base += 2 * B_K_WIDTH
    for c in range(0, B_V_WIDTH, step):
        _store_groups(_dot(xb, w_ref[:, base + c:base + c + step]), bv_ref, c // LANES)
    base += B_V_WIDTH
    for c in range(0, MIX_WIDTH, step):
        _store_groups(_dot(xb, w_ref[:, base + c:base + c + step]), gate_ref, c // LANES)
    lr_ref[...] = _dot(xb, wlr_ref[...])


def _inproj_ab(x2d, w_main, w_lr, batch, seq):
    T = x2d.shape[0]
    tm = PROJ_ROWS
    n_main = w_main.shape[1]
    nbs = seq // tm
    grp = lambda n: pl.BlockSpec((n, tm, LANES), lambda i: (0, i, 0))
    split = lambda d: pl.BlockSpec((3 * A_GROUPS, None, d, tm // d, LANES),
                                   lambda i: (0, i // nbs, 0, i % nbs, 0))
    split_shape = lambda d: jax.ShapeDtypeStruct((3 * A_GROUPS, batch, d, seq // d, LANES), BF16)
    return pl.pallas_call(
        _inproj_ab_kernel,
        grid=(T // tm,),
        in_specs=[
            pl.BlockSpec((tm, D_MODEL), lambda i: (i, 0)),
            pl.BlockSpec((D_MODEL, n_main), lambda i: (0, 0)),
            pl.BlockSpec((D_MODEL, LANES), lambda i: (0, 0)),
        ],
        out_specs=[split(d) for d in DILATIONS] + [grp(8), grp(8), grp(MIX_GROUPS),
                                                   pl.BlockSpec((tm, LANES), lambda i: (i, 0))],
        scratch_shapes=[pltpu.VMEM((A_GROUPS, tm, LANES), F32),
                        pltpu.VMEM((A_GROUPS, DILATIONS[1], tm // DILATIONS[1], LANES), F32)],
        out_shape=[split_shape(d) for d in DILATIONS] + [
            jax.ShapeDtypeStruct((8, T, LANES), BF16),
            jax.ShapeDtypeStruct((8, T, LANES), BF16),
            jax.ShapeDtypeStruct((MIX_GROUPS, T, LANES), BF16),
            jax.ShapeDtypeStruct((T, LANES), F32),
        ],
        compiler_params=_params(("parallel",)),
        name="inproj_ab",
    )(x2d, w_main, w_lr)


def _inproj_c_kernel(x_ref, w_ref, lbnd_ref, q_ref, i_ref, gate_ref, lg_ref, *, layer_idx):
    xb = x_ref[...].astype(BF16)
    lbnd = lbnd_ref[...]
    e = jnp.exp(lbnd - jnp.max(lbnd, axis=0, keepdims=True))
    sm = e / jnp.sum(e, axis=0, keepdims=True)
    lb = jnp.sum(sm[1:layer_idx + 1], axis=0, keepdims=True)
    step = 4 * LANES
    for c in range(0, C_WIDTH, step):
        q = _dot(xb, w_ref[:, c:c + step])
        _store_groups(_silu(q) * (C_HEAD_DIM ** -0.5), q_ref, c // LANES)
    for d in range(2):
        base = (1 + d) * C_WIDTH
        for c in range(0, C_WIDTH, step):
            z = _dot(xb, w_ref[:, base + c:base + c + step])
            lbc = lb[:, c:c + step]
            f = lbc + (1.0 - lbc) * _sigmoid(z)
            _store_groups(jnp.log2(f), lg_ref, (d * C_WIDTH + c) // LANES)
    for c in range(0, C_WIDTH, step):
        _store_groups(_dot(xb, w_ref[:, 3 * C_WIDTH + c:3 * C_WIDTH + c + step]), i_ref, c // LANES)
    for c in range(0, C_WIDTH, step):
        _store_groups(_dot(xb, w_ref[:, 4 * C_WIDTH + c:4 * C_WIDTH + c + step]), gate_ref, c // LANES)


def _inproj_c(x2d, w, lower_bounds, layer_idx):
    T = x2d.shape[0]
    tm = PROJ_ROWS // 2
    grp = lambda n: pl.BlockSpec((n, tm, LANES), lambda i: (0, i, 0))
    return pl.pallas_call(
        functools.partial(_inproj_c_kernel, layer_idx=layer_idx),
        grid=(T // tm,),
        in_specs=[
            pl.BlockSpec((tm, D_MODEL), lambda i: (i, 0)),
            pl.BlockSpec((D_MODEL, 5 * C_WIDTH), lambda i: (0, 0)),
            pl.BlockSpec((DEPTH, C_WIDTH), lambda i: (0, 0)),
        ],
        out_specs=[grp(C_HEADS), grp(C_HEADS), grp(C_HEADS), grp(2 * C_HEADS)],
        out_shape=[
            jax.ShapeDtypeStruct((C_HEADS, T, LANES), BF16),
            jax.ShapeDtypeStruct((C_HEADS, T, LANES), BF16),
            jax.ShapeDtypeStruct((C_HEADS, T, LANES), BF16),
            jax.ShapeDtypeStruct((2 * C_HEADS, T, LANES), F32),
        ],
        compiler_params=_params(("parallel",)),
        name="inproj_c",
    )(x2d, w, lower_bounds)


def _t5_bucket(rel):
    half = REL_BUCKETS // 2
    max_exact = half // 2
    n = np.abs(rel)
    large = max_exact + (np.log(np.maximum(n, 1) / max_exact)
                         / np.log(REL_MAX_DISTANCE / max_exact) * (half - max_exact)).astype(np.int32)
    large = np.minimum(large, half - 1)
    return np.where(rel > 0, half, 0) + np.where(n < max_exact, n, large)


def _bias_tables(rel_bias, dilation):
    offs = np.arange(-A_HALF_STEPS, A_HALF_STEPS + 1)
    vals = rel_bias.astype(F32)[_t5_bucket(offs * dilation)] * LOG2E
    period = ATT_K + ATT_Q
    ext = jnp.concatenate([vals, jnp.full((period - vals.shape[0], A_HEADS), NEG, F32)], axis=0).T
    flat = jnp.broadcast_to(ext[:, None, :], (A_HEADS, ATT_Q, period)).reshape(A_HEADS, ATT_Q * period)
    tbl = flat[:, :ATT_Q * (period - 1)].reshape(A_HEADS, ATT_Q, period - 1)[:, :, :ATT_K]
    col = np.arange(ATT_K)
    before, after = col < A_HALF_STEPS, col >= ATT_K - A_HALF_STEPS
    variants = []
    for mask in (np.zeros_like(before), before, after, before | after):
        t = jnp.where(mask[None, None, :], NEG, tbl).reshape(A_GROUPS, 2, ATT_Q, ATT_K)
        variants.append(jnp.transpose(t, (0, 2, 1, 3)).reshape(A_GROUPS, ATT_Q, 2 * ATT_K))
    return jnp.stack(variants, axis=0)


def _attn_kernel(q_ref, kp_ref, kc_ref, kn_ref, vp_ref, vc_ref, vn_ref, bias_ref, o_ref, lse_ref, *,
                 dilation, rows):
    j = pl.program_id(1)
    first = (j == 0).astype(jnp.int32)
    last = (j == pl.num_programs(1) - 1).astype(jnp.int32)
    halo = A_HALF_STEPS
    n_sub = rows // ATT_Q
    low_kv = lax.broadcasted_iota(jnp.int32, (ATT_K, LANES), 1) < A_HEAD_DIM
    low_q = lax.broadcasted_iota(jnp.int32, (ATT_Q, LANES), 1) < A_HEAD_DIM
    zero = jnp.zeros((ATT_K, LANES), BF16)
    one = jnp.ones((ATT_K, LANES), BF16)

    def window(prev_ref, cur_ref, next_ref, g, r, i):
        lo, hi = i * ATT_Q - halo, i * ATT_Q - halo + ATT_K
        parts = [prev_ref[g, r]] if lo < 0 else []
        parts.append(cur_ref[g, r, max(lo, 0):min(hi, rows), :])
        if hi > rows:
            parts.append(next_ref[g, r])
        return parts[0] if len(parts) == 1 else jnp.concatenate(parts, axis=0)

    units = [(r, i) for r in range(dilation) for i in range(n_sub)]

    def scores(g, r, i):
        variant = (first if i == 0 else 0) + (2 * last if i == n_sub - 1 else 0)
        bias = bias_ref[variant, g]
        q2 = q_ref[g, r, i * ATT_Q:(i + 1) * ATT_Q, :]
        k2 = window(kp_ref, kc_ref, kn_ref, g, r, i)
        s0 = _dot_nt(q2, jnp.where(low_kv, k2, zero)) + bias[:, :ATT_K]
        s1 = _dot_nt(q2, jnp.where(low_kv, zero, k2)) + bias[:, ATT_K:]
        return s0, s1

    def softmax(s):
        m = jnp.max(s, axis=1, keepdims=True)
        return m, jnp.exp2(s - m).astype(BF16)

    def finish(g, r, i, m0, p0, m1, p1):
        v2 = window(vp_ref, vc_ref, vn_ref, g, r, i)
        r0 = _dot(p0, jnp.where(low_kv, v2, one))
        r1 = _dot(p1, jnp.where(low_kv, one, v2))
        pv = jnp.where(low_q, r0, r1)
        denom = pltpu.roll(jnp.where(low_q, r1, r0), A_HEAD_DIM, axis=1)
        m = jnp.where(low_q, m0, m1)
        start = r + dilation * i * ATT_Q
        dst = pl.ds(start, ATT_Q) if dilation == 1 else pl.ds(start, ATT_Q, stride=dilation)
        o_ref[g, dst, :] = pv / denom
        lse_ref[g, dst, :] = m + jnp.log2(denom)

    def group_body(g, carry):
        for u0 in range(0, len(units), ATT_BATCH):
            batch = units[u0:u0 + ATT_BATCH]
            s = [scores(g, r, i) for r, i in batch]
            sm = [(softmax(s0), softmax(s1)) for s0, s1 in s]
            for (r, i), ((m0, p0), (m1, p1)) in zip(batch, sm):
                finish(g, r, i, m0, p0, m1, p1)
        return carry

    lax.fori_loop(0, A_GROUPS, group_body, 0)


def _attention_pattern(a_split, bias_tbl, dilation):
    _, batch, _, L, _ = a_split.shape
    T = batch * L * dilation
    rows = ATT_TOKENS // dilation
    halo = A_HALF_STEPS
    hb = rows // halo
    nj = L // rows
    cur = lambda part: pl.BlockSpec((A_GROUPS, None, dilation, rows, LANES), lambda b, j: (part, b, 0, j, 0))
    prev = lambda part: pl.BlockSpec((A_GROUPS, None, dilation, halo, LANES),
                                     lambda b, j: (part, b, 0, jnp.maximum(j * hb - 1, 0), 0))
    nxt = lambda part: pl.BlockSpec((A_GROUPS, None, dilation, halo, LANES),
                                    lambda b, j: (part, b, 0, jnp.minimum((j + 1) * hb, L // halo - 1), 0))
    out_spec = pl.BlockSpec((A_GROUPS, ATT_TOKENS, LANES), lambda b, j: (0, b * nj + j, 0))
    return pl.pallas_call(
        functools.partial(_attn_kernel, dilation=dilation, rows=rows),
        grid=(batch, nj),
        in_specs=[cur(0), prev(1), cur(1), nxt(1), prev(2), cur(2), nxt(2),
                  pl.BlockSpec(bias_tbl.shape, lambda b, j: (0, 0, 0, 0))],
        out_specs=[out_spec, out_spec],
        out_shape=[jax.ShapeDtypeStruct((A_GROUPS, T, LANES), F32)] * 2,
        compiler_params=_params(("parallel", "parallel")),
        name=f"dilated_attn_d{dilation}",
    )(a_split, a_split, a_split, a_split, a_split, a_split, a_split, bias_tbl)


def _scan_constants():
    t = np.arange(ROW_BLOCK)[:, None]
    r = np.arange(ROW_BLOCK)[None, :]
    same = t // CHUNK == r // CHUNK
    totals = np.repeat(np.arange(ROW_BLOCK // CHUNK), TOTAL_ROWS)[:, None] == r // CHUNK
    prefix = same & (r <= t)
    suffix = same & (r >= t)
    state_b = np.concatenate([same & (r < t), totals], axis=0)
    as_bf16 = lambda m: jnp.asarray(m.astype(np.float32), BF16)
    return as_bf16(prefix), as_bf16(suffix), as_bf16(state_b)


def _intra_masks():
    t = lax.broadcasted_iota(jnp.int32, (CHUNK, CHUNK), 0)
    s = lax.broadcasted_iota(jnp.int32, (CHUNK, CHUNK), 1)
    tb, sb = t // SUB, s // SUB
    return (tb == sb) & (s <= t), tb > sb, (tb == sb) & (s > t)


N_OPERANDS = 6


def _decayed_operands(cum_ref, q_rows, k_rows, op_ref, backward):
    per_chunk = CHUNK // SUB
    half = SUB // 2
    for b in range(ROW_BLOCK // SUB):
        r0 = b * SUB
        rows = slice(r0, r0 + SUB)
        row = lambda i: cum_ref[i:i + 1, :]
        if backward:
            mid, r_out = row(r0 + half), row(r0)
            r_in = None if b % per_chunk == per_chunk - 1 else row(r0 + SUB)
        else:
            mid, r_out = row(r0 + half - 1), row(r0 + SUB - 1)
            r_in = None if b % per_chunk == 0 else row(r0 - 1)
        p = cum_ref[rows, :]
        q, k = q_rows(rows), k_rows(rows)
        d_mid = p - mid
        q_in = q * jnp.exp2(p if r_in is None else p - r_in)
        q_dec = q_in if r_in is None else q_in * jnp.exp2(r_in)
        k_out = k * jnp.exp2(r_out - p)
        vals = [q * jnp.exp2(d_mid), k * jnp.exp2(-d_mid), q_in, k_out, q_dec]
        if not backward:
            chunk_end = row((b // per_chunk + 1) * CHUNK - 1)
            vals.append(k_out if b % per_chunk == per_chunk - 1 else k_out * jnp.exp2(chunk_end - r_out))
        for i, val in enumerate(vals):
            op_ref[i, rows, :] = val.astype(BF16)


def _intra_matrix(ci, opf, opb, masks):
    m_df, m_of, m_db = masks
    rows = slice(ci * CHUNK, (ci + 1) * CHUNK)
    a = jnp.where(m_df, _dot_nt(opf[0, rows, :], opf[1, rows, :]),
                  jnp.where(m_of, _dot_nt(opf[2, rows, :], opf[3, rows, :]),
                            jnp.where(m_db, _dot_nt(opb[0, rows, :], opb[1, rows, :]),
                                      _dot_nt(opb[2, rows, :], opb[3, rows, :]))))
    return a.astype(BF16)


def _chunk_output(ci, a, opf, opb, v, stf, stb):
    rows = slice(ci * CHUNK, (ci + 1) * CHUNK)
    q_dec = jnp.concatenate([opf[4, rows, :], opb[4, rows, :]], axis=1)
    return _dot(a, v) + _dot_nt(q_dec, jnp.concatenate([stf, stb], axis=1))


def _state_pass(n_heads, nv, get_lg, get_k, v_ref, msel, out_ref, st_ref, kdec_scr, etot_scr):
    order = range(STATE_BLOCK // CHUNK - 1, -1, -1)
    totals_per_part = ROW_BLOCK // CHUNK * TOTAL_ROWS
    for h in range(n_heads):
        for part in range(STATE_BLOCK // ROW_BLOCK):
            rows = slice(part * ROW_BLOCK, (part + 1) * ROW_BLOCK)
            lg = get_lg(h, rows)
            d = _dot_sel(msel, lg)
            kdec_scr[h, rows, :] = (get_k(h, rows, lg) * jnp.exp2(d[0:ROW_BLOCK])).astype(BF16)
            etot_scr[h, part * totals_per_part:(part + 1) * totals_per_part, :] = jnp.exp2(d[ROW_BLOCK:])
    for h in range(n_heads):
        ds = {}
        for ci in order:
            rows = slice(ci * CHUNK, (ci + 1) * CHUNK)
            ds[ci] = _dot_tn(_head_value(v_ref, h, rows, nv), kdec_scr[h, rows, :])
        st = st_ref[h]
        for ci in order:
            out_ref[ci, h] = st.astype(out_ref.dtype)
            st = st * etot_scr[h, ci * TOTAL_ROWS:ci * TOTAL_ROWS + 1, :] + ds[ci]
        st_ref[h] = st


def _head_value(v_ref, h, rows, nv):
    if nv == 1:
        return v_ref[h, rows, :]
    return jnp.concatenate([v_ref[nv * h + k, rows, :] for k in range(nv)], axis=1)


def _gla_log_gates(lr, up_ref, bias_ref, lg_scr, directions):
    for d in directions:
        cols = slice(d * B_K_WIDTH, (d + 1) * B_K_WIDTH)
        z = _dot_f32(lr, up_ref[:, cols]) + bias_ref[:, cols]
        lg = _log_sigmoid(z) * (LOG2E / B_GATE_NORMALIZER)
        for h in range(B_HEADS):
            lg_scr[d * B_HEADS + h] = lg[:, h * LANES:(h + 1) * LANES]


def _state_kernel_c(lg_ref, v_ref, msel_ref, out_ref, st_ref, kdec_scr, etot_scr):
    @pl.when(pl.program_id(1) == 0)
    def _():
        st_ref[...] = jnp.zeros_like(st_ref)

    _state_pass(C_HEADS, 1, lambda h, rows: lg_ref[h, rows, :], lambda h, rows, lg: 1.0 - jnp.exp2(lg), v_ref,
                msel_ref[...], out_ref, st_ref, kdec_scr, etot_scr)


def _state_kernel_b(lr_ref, k_ref, v_ref, up_ref, bias_ref, msel_ref, out_ref, st_ref, lg_scr, kdec_scr,
                    etot_scr):
    @pl.when(pl.program_id(1) == 0)
    def _():
        st_ref[...] = jnp.zeros_like(st_ref)

    _gla_log_gates(lr_ref[...], up_ref, bias_ref, lg_scr, directions=(1,))
    _state_pass(B_HEADS, B_VAL_DIM // LANES, lambda h, rows: lg_scr[B_HEADS + h, rows, :],
                lambda h, rows, lg: k_ref[h, rows, :].astype(F32),
                v_ref, msel_ref[...], out_ref, st_ref, kdec_scr, etot_scr)


def _reversed_block_index(blocks_per_seq):
    return lambda b, n: b * blocks_per_seq + (blocks_per_seq - 1 - n)


def _states_c(lg, v, msel, batch, seq):
    T = v.shape[1]
    nb = seq // STATE_BLOCK
    cpb = STATE_BLOCK // CHUNK
    blk = _reversed_block_index(nb)
    return pl.pallas_call(
        _state_kernel_c,
        grid=(batch, nb),
        in_specs=[
            pl.BlockSpec((C_HEADS, STATE_BLOCK, LANES), lambda b, n: (1, blk(b, n), 0)),
            pl.BlockSpec((C_HEADS, STATE_BLOCK, LANES), lambda b, n: (0, blk(b, n), 0)),
            pl.BlockSpec(msel.shape, lambda b, n: (0, 0)),
        ],
        out_specs=pl.BlockSpec((cpb, C_HEADS, C_HEAD_DIM, C_HEAD_DIM), lambda b, n: (blk(b, n), 0, 0, 0)),
        out_shape=jax.ShapeDtypeStruct((T // CHUNK, C_HEADS, C_HEAD_DIM, C_HEAD_DIM), BF16),
        scratch_shapes=[pltpu.VMEM((C_HEADS, C_HEAD_DIM, C_HEAD_DIM), F32),
                        pltpu.VMEM((C_HEADS, STATE_BLOCK, LANES), BF16),
                        pltpu.VMEM((C_HEADS, cpb * TOTAL_ROWS, LANES), F32)],
        compiler_params=_params(("parallel", "arbitrary")),
        name="hgrn_states_bwd",
    )(lg, v, msel)


def _states_b(lr, bqk, bv, up_pad, bias_pad, msel, batch, seq):
    T = lr.shape[0]
    nb = seq // STATE_BLOCK
    cpb = STATE_BLOCK // CHUNK
    blk = _reversed_block_index(nb)
    full = lambda a: pl.BlockSpec(a.shape, lambda b, n: (0,) * a.ndim)
    return pl.pallas_call(
        _state_kernel_b,
        grid=(batch, nb),
        in_specs=[
            pl.BlockSpec((STATE_BLOCK, LANES), lambda b, n: (blk(b, n), 0)),
            pl.BlockSpec((B_HEADS, STATE_BLOCK, LANES), lambda b, n: (1, blk(b, n), 0)),
            pl.BlockSpec((2 * B_HEADS, STATE_BLOCK, LANES), lambda b, n: (0, blk(b, n), 0)),
            full(up_pad), full(bias_pad), full(msel),
        ],
        out_specs=pl.BlockSpec((cpb, B_HEADS, B_VAL_DIM, B_KEY_DIM), lambda b, n: (blk(b, n), 0, 0, 0)),
        out_shape=jax.ShapeDtypeStruct((T // CHUNK, B_HEADS, B_VAL_DIM, B_KEY_DIM), BF16),
        scratch_shapes=[pltpu.VMEM((B_HEADS, B_VAL_DIM, B_KEY_DIM), F32),
                        pltpu.VMEM((2 * B_HEADS, STATE_BLOCK, LANES), F32),
                        pltpu.VMEM((B_HEADS, STATE_BLOCK, LANES), BF16),
                        pltpu.VMEM((B_HEADS, cpb * TOTAL_ROWS, LANES), F32)],
        compiler_params=_params(("parallel", "arbitrary")),
        name="gla_states_bwd",
    )(lr, bqk, bv, up_pad, bias_pad, msel)


def _project_and_norm(y_scr, wout_ref, x_ref, lng_ref, lnb_ref, out_ref):
    y = jnp.concatenate([y_scr[g] for g in range(MIX_GROUPS)], axis=1)
    z = DEEPNORM_ALPHA * x_ref[...] + _dot(y, wout_ref[...])
    out_ref[...] = _layer_norm(z, lng_ref[...], lnb_ref[...])


def _rms_gain(o, gain):
    return o * lax.rsqrt(jnp.mean(o * o, axis=-1, keepdims=True) + NORM_EPS) * gain


def _scan_heads(n_heads, nv, lg_of, q_rows_of, k_rows_of, v_ref, st_ref, sb_ref, mpre, msuf, scan_scr, finish):
    masks = _intra_masks()

    cum_scr, op_scr, a_scr, ds_scr, stf_scr = scan_scr

    @pl.when(pl.program_id(1) == 0)
    def _():
        st_ref[...] = jnp.zeros_like(st_ref)

    per_step = cum_scr.shape[0]

    def step(i, carry):
        heads = [i * per_step + j for j in range(per_step)]
        for j, h in enumerate(heads):
            cum_scr[j, 0] = _dot_sel(mpre, lg_of(h, 0))
            cum_scr[j, 1] = _dot_sel(msuf, lg_of(h, 1))
        for j, h in enumerate(heads):
            for d in range(2):
                _decayed_operands(cum_scr.at[j, d], q_rows_of(h), k_rows_of(h, d), op_scr.at[j, d], d == 1)
        units = [(j, h, ci) for j, h in enumerate(heads) for ci in range(ROW_BLOCK // CHUNK)]
        chunk_rows = lambda ci: slice(ci * CHUNK, (ci + 1) * CHUNK)
        for j, h, ci in units:
            a_scr[j, ci] = _intra_matrix(ci, op_scr.at[j, 0], op_scr.at[j, 1], masks)
            ds_scr[j, ci] = _dot_tn(_head_value(v_ref, h, chunk_rows(ci), nv),
                                    op_scr[j, 0, N_OPERANDS - 1, chunk_rows(ci), :])
        for j, h in enumerate(heads):
            st = st_ref[h]
            for ci in range(ROW_BLOCK // CHUNK):
                stf_scr[j, ci] = st.astype(BF16)
                total = cum_scr[j, 0, (ci + 1) * CHUNK - 1:(ci + 1) * CHUNK, :]
                st = st * jnp.exp2(total) + ds_scr[j, ci]
            st_ref[h] = st
        for j, h, ci in units:
            o = _chunk_output(ci, a_scr[j, ci], op_scr.at[j, 0], op_scr.at[j, 1],
                              _head_value(v_ref, h, chunk_rows(ci), nv), stf_scr[j, ci], sb_ref[ci, h])
            finish(h, chunk_rows(ci), o)
        return carry

    lax.fori_loop(0, n_heads // per_step, step, 0)


def _out_kernel_c(q_ref, i_ref, gate_ref, lgf_ref, lgb_ref, sb_ref, mpre_ref, msuf_ref, gain_ref,
                  wout_ref, x_ref, lng_ref, lnb_ref, out_ref, y_scr, st_scr, *scan_scr):
    lg_refs = (lgf_ref, lgb_ref)

    def finish(h, rows, o):
        y = _rms_gain(o, gain_ref[h]) * _silu(gate_ref[h, rows, :].astype(F32))
        y_scr[h, rows, :] = y.astype(BF16)

    _scan_heads(C_HEADS, 1,
                lambda h, d: lg_refs[d][h],
                lambda h: lambda rows: q_ref[h, rows, :].astype(F32),
                lambda h, d: lambda rows: 1.0 - jnp.exp2(lg_refs[d][h, rows, :]),
                i_ref, st_scr, sb_ref, mpre_ref[...], msuf_ref[...], scan_scr, finish)
    _project_and_norm(y_scr, wout_ref, x_ref, lng_ref, lnb_ref, out_ref)


def _out_kernel_ab(o1_ref, o2_ref, o3_ref, l1_ref, l2_ref, l3_ref, qk_ref, v_ref, gate_ref, lr_ref,
                   sb_ref, mf_ref, mb_ref, up_ref, gbias_ref, gain_ref,
                   wout_ref, x_ref, lng_ref, lnb_ref, out_ref, y_scr, lg_scr, st_scr, *scan_scr):
    for g in range(A_GROUPS):
        l1, l2, l3 = l1_ref[g], l2_ref[g], l3_ref[g]
        m = jnp.maximum(jnp.maximum(l1, l2), l3)
        e1, e2, e3 = jnp.exp2(l1 - m), jnp.exp2(l2 - m), jnp.exp2(l3 - m)
        num = e1 * o1_ref[g].astype(F32) + e2 * o2_ref[g].astype(F32) + e3 * o3_ref[g].astype(F32)
        oa = num / (e1 + e2 + e3)
        y_scr[g] = (oa * _silu(gate_ref[g].astype(F32))).astype(BF16)

    _gla_log_gates(lr_ref[...], up_ref, gbias_ref, lg_scr, directions=(0, 1))
    nv = B_VAL_DIM // LANES

    def finish(h, rows, o):
        gain = jnp.concatenate([gain_ref[nv * h + k] for k in range(nv)], axis=1)
        o = _rms_gain(o, gain)
        for kk in range(nv):
            grp = A_GROUPS + nv * h + kk
            y = o[:, kk * LANES:(kk + 1) * LANES] * _silu(gate_ref[grp, rows, :].astype(F32))
            y_scr[grp, rows, :] = y.astype(BF16)

    _scan_heads(B_HEADS, nv,
                lambda h, d: lg_scr[d * B_HEADS + h],
                lambda h: lambda rows: qk_ref[h, rows, :].astype(F32),
                lambda h, d: lambda rows: qk_ref[B_HEADS + h, rows, :].astype(F32),
                v_ref, st_scr, sb_ref, mf_ref[...], mb_ref[...], scan_scr, finish)
    _project_and_norm(y_scr, wout_ref, x_ref, lng_ref, lnb_ref, out_ref)


def _scan_scratch(tb, n_heads, v_dim, k_dim):
    cpb = tb // CHUNK
    per_step = min(HEADS_PER_STEP, n_heads)
    assert n_heads % per_step == 0
    return [pltpu.VMEM((per_step, 2, tb, LANES), F32),
            pltpu.VMEM((per_step, 2, N_OPERANDS, tb, LANES), BF16),
            pltpu.VMEM((per_step, cpb, CHUNK, CHUNK), BF16),
            pltpu.VMEM((per_step, cpb, v_dim, k_dim), F32),
            pltpu.VMEM((per_step, cpb, v_dim, k_dim), BF16)]


def _row_specs(tb, nb):
    grp = lambda n, part=0: pl.BlockSpec((n, tb, LANES), lambda b, i: (part, b * nb + i, 0))
    full = lambda a: pl.BlockSpec(a.shape, lambda b, i: (0,) * a.ndim)
    rows = lambda width: pl.BlockSpec((tb, width), lambda b, i: (b * nb + i, 0))
    return grp, full, rows


def _output_c(cq, ci_, gate, lg, sb, mf, mb, gain, wout, x2d, lng, lnb, batch, seq):
    T = x2d.shape[0]
    tb = ROW_BLOCK
    nb = seq // tb
    cpb = tb // CHUNK
    grp, full, rows = _row_specs(tb, nb)
    st = pl.BlockSpec((cpb, C_HEADS, C_HEAD_DIM, C_HEAD_DIM), lambda b, i: (b * nb + i, 0, 0, 0))
    return pl.pallas_call(
        _out_kernel_c,
        grid=(batch, nb),
        in_specs=[grp(C_HEADS), grp(C_HEADS), grp(C_HEADS), grp(C_HEADS, 0), grp(C_HEADS, 1), st,
                  full(mf), full(mb), full(gain), full(wout), rows(D_MODEL), full(lng), full(lnb)],
        out_specs=rows(D_MODEL),
        out_shape=jax.ShapeDtypeStruct((T, D_MODEL), F32),
        scratch_shapes=[pltpu.VMEM((MIX_GROUPS, tb, LANES), BF16),
                        pltpu.VMEM((C_HEADS, C_HEAD_DIM, C_HEAD_DIM), F32)]
        + _scan_scratch(tb, C_HEADS, C_HEAD_DIM, C_HEAD_DIM),
        compiler_params=_params(("parallel", "arbitrary")),
        name="hgrn_output",
    )(cq, ci_, gate, lg, lg, sb, mf, mb, gain, wout, x2d, lng, lnb)


def _output_ab(attn, bqk, bv, gate, lr, sb, mf, mb, up_pad, gbias_pad, gain, wout, x2d, lng, lnb, batch, seq):
    T = x2d.shape[0]
    tb = ROW_BLOCK
    nb = seq // tb
    cpb = tb // CHUNK
    grp, full, rows = _row_specs(tb, nb)
    st = pl.BlockSpec((cpb, B_HEADS, B_VAL_DIM, B_KEY_DIM), lambda b, i: (b * nb + i, 0, 0, 0))
    (o1, l1), (o2, l2), (o3, l3) = attn
    return pl.pallas_call(
        _out_kernel_ab,
        grid=(batch, nb),
        in_specs=[grp(A_GROUPS)] * 6 + [grp(2 * B_HEADS), grp(8), grp(MIX_GROUPS), rows(LANES), st,
                                        full(mf), full(mb), full(up_pad), full(gbias_pad), full(gain),
                                        full(wout), rows(D_MODEL), full(lng), full(lnb)],
        out_specs=rows(D_MODEL),
        out_shape=jax.ShapeDtypeStruct((T, D_MODEL), F32),
        scratch_shapes=[pltpu.VMEM((MIX_GROUPS, tb, LANES), BF16),
                        pltpu.VMEM((2 * B_HEADS, tb, LANES), F32),
                        pltpu.VMEM((B_HEADS, B_VAL_DIM, B_KEY_DIM), F32)]
        + _scan_scratch(tb, B_HEADS, B_VAL_DIM, B_KEY_DIM),
        compiler_params=_params(("parallel", "arbitrary")),
        name="ab_output",
    )(o1, o2, o3, l1, l2, l3, bqk, bv, gate, lr, sb, mf, mb, up_pad, gbias_pad, gain, wout, x2d, lng, lnb)


def _layer_ab(x2d, batch, seq, w_in, gate_up, gate_bias, norm_gain, w_out, bias_tbls, lng, lnb, consts):
    mf, mb, msel = consts
    sizes = (3 * A_WIDTH + 2 * B_K_WIDTH + B_V_WIDTH, 2 * B_GATE_RANK, MIX_WIDTH)
    o0, o1 = sizes[0], sizes[0] + sizes[1]
    w_main = jnp.concatenate([w_in[:, :o0], w_in[:, o1:]], axis=1).astype(BF16)
    w_lr = jnp.pad(w_in[:, o0:o1], ((0, 0), (0, LANES - sizes[1]))).astype(BF16)
    up = gate_up.astype(F32)
    up_pad = jnp.zeros((LANES, 2 * B_K_WIDTH), F32)
    up_pad = up_pad.at[0:B_GATE_RANK, 0:B_K_WIDTH].set(up[0])
    up_pad = up_pad.at[B_GATE_RANK:2 * B_GATE_RANK, B_K_WIDTH:].set(up[1])
    gbias_pad = gate_bias.astype(F32).reshape(1, 2 * B_K_WIDTH)

    *a_splits, bqk, bv, gate, lr = _inproj_ab(x2d, w_main, w_lr, batch, seq)
    attn = [_attention_pattern(a, tbl, d) for a, tbl, d in zip(a_splits, bias_tbls, DILATIONS)]
    sb = _states_b(lr, bqk, bv, up_pad, gbias_pad, msel, batch, seq)
    gain = norm_gain.astype(F32).reshape(B_V_WIDTH // LANES, 1, LANES)
    wout = w_out.astype(BF16)
    return _output_ab(attn, bqk, bv, gate, lr, sb, mf, mb, up_pad, gbias_pad, gain, wout, x2d, lng, lnb,
                      batch, seq)


def _layer_c(x2d, batch, seq, w_in, lower_bounds, layer_idx, norm_gain, w_out, lng, lnb, consts):
    mf, mb, msel = consts
    cq, ci_, gate, lg = _inproj_c(x2d, w_in.astype(BF16), lower_bounds.astype(F32), layer_idx)
    sb = _states_c(lg, ci_, msel, batch, seq)
    gain = norm_gain.astype(F32).reshape(C_HEADS, 1, LANES)
    wout = w_out.astype(BF16)
    return _output_c(cq, ci_, gate, lg, sb, mf, mb, gain, wout, x2d, lng, lnb, batch, seq)


def kernel(x, w_in_ab, gla_gate_up, gla_gate_bias, gla_norm, w_out_ab, w_in_c, hgrn_lower_bounds, hgrn_norm,
           w_out_c, rel_bias, ln_gain, ln_bias):
    batch, seq, _ = x.shape
    assert seq % ATT_TOKENS == 0 and seq % ROW_BLOCK == 0 and seq % STATE_BLOCK == 0 and seq % PROJ_ROWS == 0
    consts = _scan_constants()
    bias_tbls = [_bias_tables(rel_bias, d) for d in DILATIONS]
    x2d = x.astype(F32).reshape(batch * seq, D_MODEL)
    for layer in range(DEPTH):
        lng = ln_gain[layer].astype(F32).reshape(1, D_MODEL)
        lnb = ln_bias[layer].astype(F32).reshape(1, D_MODEL)
        if layer % 2 == 0:
            e = layer // 2
            x2d = _layer_ab(x2d, batch, seq, w_in_ab[e], gla_gate_up[e], gla_gate_bias[e], gla_norm[e],
                            w_out_ab[e], bias_tbls, lng, lnb, consts)
        else:
            o = layer // 2
            x2d = _layer_c(x2d, batch, seq, w_in_c[o], hgrn_lower_bounds, layer, hgrn_norm[o], w_out_c[o],
                           lng, lnb, consts)
    return x2d.reshape(batch, seq, D_MODEL).astype(x.dtype)
```

```python
import functools

import numpy as np
import jax
import jax.numpy as jnp
from jax import lax
from jax.experimental import pallas as pl
from jax.experimental.pallas import tpu as pltpu

F32 = jnp.float32
BF16 = jnp.bfloat16

LANES = 128
VMEM_LIMIT = 56 * 1024 * 1024

D_MODEL = 1024
DEPTH = 4
A_HEADS = 8
A_HEAD_DIM = 64
A_WIDTH = A_HEADS * A_HEAD_DIM
A_GROUPS = A_WIDTH // LANES
DILATIONS = (1, 4, 16)
A_HALF_STEPS = 64
REL_BUCKETS = 32
REL_MAX_DISTANCE = 1024
B_HEADS = 4
B_KEY_DIM = 128
B_VAL_DIM = 256
B_K_WIDTH = B_HEADS * B_KEY_DIM
B_V_WIDTH = B_HEADS * B_VAL_DIM
B_GATE_RANK = 16
B_GATE_NORMALIZER = 16.0
C_HEADS = 12
C_HEAD_DIM = 128
C_WIDTH = C_HEADS * C_HEAD_DIM
MIX_WIDTH = A_WIDTH + B_V_WIDTH
MIX_GROUPS = MIX_WIDTH // LANES
NORM_EPS = 1e-5
DEEPNORM_ALPHA = (2 * DEPTH) ** 0.25

NEG = -1e30
LOG2E = 1.4426950408889634
ATT_Q_SCALE = A_HEAD_DIM ** -0.5 * LOG2E

CHUNK = 64
SUB = 32
ATT_Q = 128
ATT_K = ATT_Q + 2 * A_HALF_STEPS
ATT_TOKENS = 2048
ATT_BATCH = 8
ROW_BLOCK = 256
STATE_BLOCK = 1024
TOTAL_ROWS = 8
HEADS_PER_STEP = 12
PROJ_ROWS = 512


def _params(sem):
    return pltpu.CompilerParams(dimension_semantics=sem, vmem_limit_bytes=VMEM_LIMIT)


def _dot(a, b):
    return jnp.dot(a, b, preferred_element_type=F32)


def _dot_nt(a, b):
    return lax.dot_general(a, b, (((1,), (1,)), ((), ())), preferred_element_type=F32)


def _dot_tn(a, b):
    return lax.dot_general(a, b, (((0,), (0,)), ((), ())), preferred_element_type=F32)


def _split2(x):
    hi = x.astype(BF16)
    lo = (x - hi.astype(F32)).astype(BF16)
    return hi, lo


def _dot_sel(m01, x):
    n = x.shape[1]
    hi, lo = _split2(x)
    both = _dot(m01, jnp.concatenate([hi, lo], axis=1))
    return both[:, :n] + both[:, n:]


def _dot_f32(a, b):
    ah, al = _split2(a)
    bh, bl = _split2(b)
    return _dot(ah, bh) + _dot(ah, bl) + _dot(al, bh)


def _sigmoid(z):
    return 1.0 / (1.0 + jnp.exp(-z))


def _silu(z):
    return z * _sigmoid(z)


def _log_sigmoid(z):
    return jnp.minimum(z, 0.0) - jnp.log(1.0 + jnp.exp(-jnp.abs(z)))


def _layer_norm(z, g, b):
    mu = jnp.mean(z, axis=-1, keepdims=True)
    zc = z - mu
    var = jnp.mean(zc * zc, axis=-1, keepdims=True)
    return zc * lax.rsqrt(var + NORM_EPS) * g + b


def _store_groups(acc, out_ref, g0):
    for k in range(acc.shape[1] // LANES):
        out_ref[g0 + k] = acc[:, k * LANES:(k + 1) * LANES].astype(out_ref.dtype)


def _inproj_ab_kernel(x_ref, w_ref, wlr_ref, a1_ref, a4_ref, a16_ref, bqk_ref, bv_ref, gate_ref, lr_ref,
                      nat_scr, split_scr):
    xb = x_ref[...].astype(BF16)
    step = 4 * LANES
    tm = x_ref.shape[0]
    ratio = DILATIONS[1]
    for c in range(0, 3 * A_WIDTH, step):
        acc = _dot(xb, w_ref[:, c:c + step])
        if c == 0:
            acc = acc * ATT_Q_SCALE
        g0 = c // LANES
        for k in range(A_GROUPS):
            slab = acc[:, k * LANES:(k + 1) * LANES]
            nat_scr[k] = slab
            a1_ref[g0 + k, 0] = slab.astype(BF16)
        for k in range(A_GROUPS):
            for r in range(ratio):
                part = nat_scr[k, pl.ds(r, tm // ratio, stride=ratio), :]
                split_scr[k, r] = part
                a4_ref[g0 + k, r] = part.astype(BF16)
            for r in range(ratio):
                for r2 in range(ratio):
                    part = split_scr[k, r, pl.ds(r2, tm // ratio // ratio, stride=ratio), :]
                    a16_ref[g0 + k, r + ratio * r2] = part.astype(BF16)
    base = 3 * A_WIDTH
    for c in range(0, 2 * B_K_WIDTH, step):
        acc = _dot(xb, w_ref[:, base + c:base + c + step])
        if c < B_K_WIDTH:
            acc = acc * (B_KEY_DIM ** -0.5)
        _store_groups(acc, bqk_ref, c // LANES)
    base += 2 * B_K_WIDTH
    for c in range(0, B_V_WIDTH, step):
        _store_groups(_dot(xb, w_ref[:, base + c:base + c + step]), bv_ref, c // LANES)
    base += B_V_WIDTH
    for c in range(0, MIX_WIDTH, step):
        _store_groups(_dot(xb, w_ref[:, base + c:base + c + step]), gate_ref, c // LANES)
    lr_ref[...] = _dot(xb, wlr_ref[...])


def _inproj_ab(x2d, w_main, w_lr, batch, seq):
    T = x2d.shape[0]
    tm = PROJ_ROWS
    n_main = w_main.shape[1]
    nbs = seq // tm
    grp = lambda n: pl.BlockSpec((n, tm, LANES), lambda i: (0, i, 0))
    split = lambda d: pl.BlockSpec((3 * A_GROUPS, None, d, tm // d, LANES),
                                   lambda i: (0, i // nbs, 0, i % nbs, 0))
    split_shape = lambda d: jax.ShapeDtypeStruct((3 * A_GROUPS, batch, d, seq // d, LANES), BF16)
    return pl.pallas_call(
        _inproj_ab_kernel,
        grid=(T // tm,),
        in_specs=[
            pl.BlockSpec((tm, D_MODEL), lambda i: (i, 0)),
            pl.BlockSpec((D_MODEL, n_main), lambda i: (0, 0)),
            pl.BlockSpec((D_MODEL, LANES), lambda i: (0, 0)),
        ],
        out_specs=[split(d) for d in DILATIONS] + [grp(8), grp(8), grp(MIX_GROUPS),
                                                   pl.BlockSpec((tm, LANES), lambda i: (i, 0))],
        scratch_shapes=[pltpu.VMEM((A_GROUPS, tm, LANES), F32),
                        pltpu.VMEM((A_GROUPS, DILATIONS[1], tm // DILATIONS[1], LANES), F32)],
        out_shape=[split_shape(d) for d in DILATIONS] + [
            jax.ShapeDtypeStruct((8, T, LANES), BF16),
            jax.ShapeDtypeStruct((8, T, LANES), BF16),
            jax.ShapeDtypeStruct((MIX_GROUPS, T, LANES), BF16),
            jax.ShapeDtypeStruct((T, LANES), F32),
        ],
        compiler_params=_params(("parallel",)),
        name="inproj_ab",
    )(x2d, w_main, w_lr)


def _inproj_c_kernel(x_ref, w_ref, lbnd_ref, q_ref, i_ref, gate_ref, lg_ref, *, layer_idx):
    xb = x_ref[...].astype(BF16)
    lbnd = lbnd_ref[...]
    e = jnp.exp(lbnd - jnp.max(lbnd, axis=0, keepdims=True))
    sm = e / jnp.sum(e, axis=0, keepdims=True)
    lb = jnp.sum(sm[1:layer_idx + 1], axis=0, keepdims=True)
    step = 4 * LANES
    for c in range(0, C_WIDTH, step):
        q = _dot(xb, w_ref[:, c:c + step])
        _store_groups(_silu(q) * (C_HEAD_DIM ** -0.5), q_ref, c // LANES)
    for d in range(2):
        base = (1 + d) * C_WIDTH
        for c in range(0, C_WIDTH, step):
            z = _dot(xb, w_ref[:, base + c:base + c + step])
            lbc = lb[:, c:c + step]
            f = lbc + (1.0 - lbc) * _sigmoid(z)
            _store_groups(jnp.log2(f), lg_ref, (d * C_WIDTH + c) // LANES)
    for c in range(0, C_WIDTH, step):
        _store_groups(_dot(xb, w_ref[:, 3 * C_WIDTH + c:3 * C_WIDTH + c + step]), i_ref, c // LANES)
    for c in range(0, C_WIDTH, step):
        _store_groups(_dot(xb, w_ref[:, 4 * C_WIDTH + c:4 * C_WIDTH + c + step]), gate_ref, c // LANES)


def _inproj_c(x2d, w, lower_bounds, layer_idx):
    T = x2d.shape[0]
    tm = PROJ_ROWS // 2
    grp = lambda n: pl.BlockSpec((n, tm, LANES), lambda i: (0, i, 0))
    return pl.pallas_call(
        functools.partial(_inproj_c_kernel, layer_idx=layer_idx),
        grid=(T // tm,),
        in_specs=[
            pl.BlockSpec((tm, D_MODEL), lambda i: (i, 0)),
            pl.BlockSpec((D_MODEL, 5 * C_WIDTH), lambda i: (0, 0)),
            pl.BlockSpec((DEPTH, C_WIDTH), lambda i: (0, 0)),
        ],
        out_specs=[grp(C_HEADS), grp(C_HEADS), grp(C_HEADS), grp(2 * C_HEADS)],
        out_shape=[
            jax.ShapeDtypeStruct((C_HEADS, T, LANES), BF16),
            jax.ShapeDtypeStruct((C_HEADS, T, LANES), BF16),
            jax.ShapeDtypeStruct((C_HEADS, T, LANES), BF16),
            jax.ShapeDtypeStruct((2 * C_HEADS, T, LANES), F32),
        ],
        compiler_params=_params(("parallel",)),
        name="inproj_c",
    )(x2d, w, lower_bounds)


def _t5_bucket(rel):
    half = REL_BUCKETS // 2
    max_exact = half // 2
    n = np.abs(rel)
    large = max_exact + (np.log(np.maximum(n, 1) / max_exact)
                         / np.log(REL_MAX_DISTANCE / max_exact) * (half - max_exact)).astype(np.int32)
    large = np.minimum(large, half - 1)
    return np.where(rel > 0, half, 0) + np.where(n < max_exact, n, large)


def _bias_tables(rel_bias, dilation):
    offs = np.arange(-A_HALF_STEPS, A_HALF_STEPS + 1)
    vals = rel_bias.astype(F32)[_t5_bucket(offs * dilation)] * LOG2E
    period = ATT_K + ATT_Q
    ext = jnp.concatenate([vals, jnp.full((period - vals.shape[0], A_HEADS), NEG, F32)], axis=0).T
    flat = jnp.broadcast_to(ext[:, None, :], (A_HEADS, ATT_Q, period)).reshape(A_HEADS, ATT_Q * period)
    tbl = flat[:, :ATT_Q * (period - 1)].reshape(A_HEADS, ATT_Q, period - 1)[:, :, :ATT_K]
    col = np.arange(ATT_K)
    before, after = col < A_HALF_STEPS, col >= ATT_K - A_HALF_STEPS
    variants = []
    for mask in (np.zeros_like(before), before, after, before | after):
        t = jnp.where(mask[None, None, :], NEG, tbl).reshape(A_GROUPS, 2, ATT_Q, ATT_K)
        variants.append(jnp.transpose(t, (0, 2, 1, 3)).reshape(A_GROUPS, ATT_Q, 2 * ATT_K))
    return jnp.stack(variants, axis=0)


def _attn_kernel(q_ref, kp_ref, kc_ref, kn_ref, vp_ref, vc_ref, vn_ref, bias_ref, o_ref, lse_ref, *,
                 dilation, rows):
    j = pl.program_id(1)
    first = (j == 0).astype(jnp.int32)
    last = (j == pl.num_programs(1) - 1).astype(jnp.int32)
    halo = A_HALF_STEPS
    n_sub = rows // ATT_Q
    low_kv = lax.broadcasted_iota(jnp.int32, (ATT_K, LANES), 1) < A_HEAD_DIM
    low_q = lax.broadcasted_iota(jnp.int32, (ATT_Q, LANES), 1) < A_HEAD_DIM
    zero = jnp.zeros((ATT_K, LANES), BF16)
    one = jnp.ones((ATT_K, LANES), BF16)

    def window(prev_ref, cur_ref, next_ref, g, r, i):
        lo, hi = i * ATT_Q - halo, i * ATT_Q - halo + ATT_K
        parts = [prev_ref[g, r]] if lo < 0 else []
        parts.append(cur_ref[g, r, max(lo, 0):min(hi, rows), :])
        if hi > rows:
            parts.append(next_ref[g, r])
        return parts[0] if len(parts) == 1 else jnp.concatenate(parts, axis=0)

    units = [(r, i) for r in range(dilation) for i in range(n_sub)]

    def scores(g, r, i):
        variant = (first if i == 0 else 0) + (2 * last if i == n_sub - 1 else 0)
        bias = bias_ref[variant, g]
        q2 = q_ref[g, r, i * ATT_Q:(i + 1) * ATT_Q, :]
        k2 = window(kp_ref, kc_ref, kn_ref, g, r, i)
        s0 = _dot_nt(q2, jnp.where(low_kv, k2, zero)) + bias[:, :ATT_K]
        s1 = _dot_nt(q2, jnp.where(low_kv, zero, k2)) + bias[:, ATT_K:]
        return s0, s1

    def softmax(s):
        m = jnp.max(s, axis=1, keepdims=True)
        return m, jnp.exp2(s - m).astype(BF16)

    def finish(g, r, i, m0, p0, m1, p1):
        v2 = window(vp_ref, vc_ref, vn_ref, g, r, i)
        r0 = _dot(p0, jnp.where(low_kv, v2, one))
        r1 = _dot(p1, jnp.where(low_kv, one, v2))
        pv = jnp.where(low_q, r0, r1)
        denom = pltpu.roll(jnp.where(low_q, r1, r0), A_HEAD_DIM, axis=1)
        m = jnp.where(low_q, m0, m1)
        start = r + dilation * i * ATT_Q
        dst = pl.ds(start, ATT_Q) if dilation == 1 else pl.ds(start, ATT_Q, stride=dilation)
        o_ref[g, dst, :] = pv / denom
        lse_ref[g, dst, :] = m + jnp.log2(denom)

    def group_body(g, carry):
        for u0 in range(0, len(units), ATT_BATCH):
            batch = units[u0:u0 + ATT_BATCH]
            s = [scores(g, r, i) for r, i in batch]
            sm = [(softmax(s0), softmax(s1)) for s0, s1 in s]
            for (r, i), ((m0, p0), (m1, p1)) in zip(batch, sm):
                finish(g, r, i, m0, p0, m1, p1)
        return carry

    lax.fori_loop(0, A_GROUPS, group_body, 0)


def _attention_pattern(a_split, bias_tbl, dilation):
    _, batch, _, L, _ = a_split.shape
    T = batch * L * dilation
    rows = ATT_TOKENS // dilation
    halo = A_HALF_STEPS
    hb = rows // halo
    nj = L // rows
    cur = lambda part: pl.BlockSpec((A_GROUPS, None, dilation, rows, LANES), lambda b, j: (part, b, 0, j, 0))
    prev = lambda part: pl.BlockSpec((A_GROUPS, None, dilation, halo, LANES),
                                     lambda b, j: (part, b, 0, jnp.maximum(j * hb - 1, 0), 0))
    nxt = lambda part: pl.BlockSpec((A_GROUPS, None, dilation, halo, LANES),
                                    lambda b, j: (part, b, 0, jnp.minimum((j + 1) * hb, L // halo - 1), 0))
    out_spec = pl.BlockSpec((A_GROUPS, ATT_TOKENS, LANES), lambda b, j: (0, b * nj + j, 0))
    return pl.pallas_call(
        functools.partial(_attn_kernel, dilation=dilation, rows=rows),
        grid=(batch, nj),
        in_specs=[cur(0), prev(1), cur(1), nxt(1), prev(2), cur(2), nxt(2),
                  pl.BlockSpec(bias_tbl.shape, lambda b, j: (0, 0, 0, 0))],
        out_specs=[out_spec, out_spec],
        out_shape=[jax.ShapeDtypeStruct((A_GROUPS, T, LANES), F32)] * 2,
        compiler_params=_params(("parallel", "parallel")),
        name=f"dilated_attn_d{dilation}",
    )(a_split, a_split, a_split, a_split, a_split, a_split, a_split, bias_tbl)


def _scan_constants():
    t = np.arange(ROW_BLOCK)[:, None]
    r = np.arange(ROW_BLOCK)[None, :]
    same = t // CHUNK == r // CHUNK
    totals = np.repeat(np.arange(ROW_BLOCK // CHUNK), TOTAL_ROWS)[:, None] == r // CHUNK
    prefix = same & (r <= t)
    suffix = same & (r >= t)
    state_b = np.concatenate([same & (r < t), totals], axis=0)
    as_bf16 = lambda m: jnp.asarray(m.astype(np.float32), BF16)
    return as_bf16(prefix), as_bf16(suffix), as_bf16(state_b)


def _intra_masks():
    t = lax.broadcasted_iota(jnp.int32, (CHUNK, CHUNK), 0)
    s = lax.broadcasted_iota(jnp.int32, (CHUNK, CHUNK), 1)
    tb, sb = t // SUB, s // SUB
    return (tb == sb) & (s <= t), tb > sb, (tb == sb) & (s > t)


N_OPERANDS = 6


def _decayed_operands(cum_ref, q_rows, k_rows, op_ref, backward):
    per_chunk = CHUNK // SUB
    half = SUB // 2
    for b in range(ROW_BLOCK // SUB):
        r0 = b * SUB
        rows = slice(r0, r0 + SUB)
        row = lambda i: cum_ref[i:i + 1, :]
        if backward:
            mid, r_out = row(r0 + half), row(r0)
            r_in = None if b % per_chunk == per_chunk - 1 else row(r0 + SUB)
        else:
            mid, r_out = row(r0 + half - 1), row(r0 + SUB - 1)
            r_in = None if b % per_chunk == 0 else row(r0 - 1)
        p = cum_ref[rows, :]
        q, k = q_rows(rows), k_rows(rows)
        d_mid = p - mid
        q_in = q * jnp.exp2(p if r_in is None else p - r_in)
        q_dec = q_in if r_in is None else q_in * jnp.exp2(r_in)
        k_out = k * jnp.exp2(r_out - p)
        vals = [q * jnp.exp2(d_mid), k * jnp.exp2(-d_mid), q_in, k_out, q_dec]
        if not backward:
            chunk_end = row((b // per_chunk + 1) * CHUNK - 1)
            vals.append(k_out if b % per_chunk == per_chunk - 1 else k_out * jnp.exp2(chunk_end - r_out))
        for i, val in enumerate(vals):
            op_ref[i, rows, :] = val.astype(BF16)


def _intra_matrix(ci, opf, opb, masks):
    m_df, m_of, m_db = masks
    rows = slice(ci * CHUNK, (ci + 1) * CHUNK)
    a = jnp.where(m_df, _dot_nt(opf[0, rows, :], opf[1, rows, :]),
                  jnp.where(m_of, _dot_nt(opf[2, rows, :], opf[3, rows, :]),
                            jnp.where(m_db, _dot_nt(opb[0, rows, :], opb[1, rows, :]),
                                      _dot_nt(opb[2, rows, :], opb[3, rows, :]))))
    return a.astype(BF16)


def _chunk_output(ci, a, opf, opb, v, stf, stb):
    rows = slice(ci * CHUNK, (ci + 1) * CHUNK)
    q_dec = jnp.concatenate([opf[4, rows, :], opb[4, rows, :]], axis=1)
    return _dot(a, v) + _dot_nt(q_dec, jnp.concatenate([stf, stb], axis=1))


def _state_pass(n_heads, nv, get_lg, get_k, v_ref, msel, out_ref, st_ref, kdec_scr, etot_scr):
    order = range(STATE_BLOCK // CHUNK - 1, -1, -1)
    totals_per_part = ROW_BLOCK // CHUNK * TOTAL_ROWS
    for h in range(n_heads):
        for part in range(STATE_BLOCK // ROW_BLOCK):
            rows = slice(part * ROW_BLOCK, (part + 1) * ROW_BLOCK)
            lg = get_lg(h, rows)
            d = _dot_sel(msel, lg)
            kdec_scr[h, rows, :] = (get_k(h, rows, lg) * jnp.exp2(d[0:ROW_BLOCK])).astype(BF16)
            etot_scr[h, part * totals_per_part:(part + 1) * totals_per_part, :] = jnp.exp2(d[ROW_BLOCK:])
    for h in range(n_heads):
        ds = {}
        for ci in order:
            rows = slice(ci * CHUNK, (ci + 1) * CHUNK)
            ds[ci] = _dot_tn(_head_value(v_ref, h, rows, nv), kdec_scr[h, rows, :])
        st = st_ref[h]
        for ci in order:
            out_ref[ci, h] = st.astype(out_ref.dtype)
            st = st * etot_scr[h, ci * TOTAL_ROWS:ci * TOTAL_ROWS + 1, :] + ds[ci]
        st_ref[h] = st


def _head_value(v_ref, h, rows, nv):
    if nv == 1:
        return v_ref[h, rows, :]
    return jnp.concatenate([v_ref[nv * h + k, rows, :] for k in range(nv)], axis=1)


def _gla_log_gates(lr, up_ref, bias_ref, lg_scr, directions):
    for d in directions:
        cols = slice(d * B_K_WIDTH, (d + 1) * B_K_WIDTH)
        z = _dot_f32(lr, up_ref[:, cols]) + bias_ref[:, cols]
        lg = _log_sigmoid(z) * (LOG2E / B_GATE_NORMALIZER)
        for h in range(B_HEADS):
            lg_scr[d * B_HEADS + h] = lg[:, h * LANES:(h + 1) * LANES]


def _state_kernel_c(lg_ref, v_ref, msel_ref, out_ref, st_ref, kdec_scr, etot_scr):
    @pl.when(pl.program_id(1) == 0)
    def _():
        st_ref[...] = jnp.zeros_like(st_ref)

    _state_pass(C_HEADS, 1, lambda h, rows: lg_ref[h, rows, :], lambda h, rows, lg: 1.0 - jnp.exp2(lg), v_ref,
                msel_ref[...], out_ref, st_ref, kdec_scr, etot_scr)


def _state_kernel_b(lr_ref, k_ref, v_ref, up_ref, bias_ref, msel_ref, out_ref, st_ref, lg_scr, kdec_scr,
                    etot_scr):
    @pl.when(pl.program_id(1) == 0)
    def _():
        st_ref[...] = jnp.zeros_like(st_ref)

    _gla_log_gates(lr_ref[...], up_ref, bias_ref, lg_scr, directions=(1,))
    _state_pass(B_HEADS, B_VAL_DIM // LANES, lambda h, rows: lg_scr[B_HEADS + h, rows, :],
                lambda h, rows, lg: k_ref[h, rows, :].astype(F32),
                v_ref, msel_ref[...], out_ref, st_ref, kdec_scr, etot_scr)


def _reversed_block_index(blocks_per_seq):
    return lambda b, n: b * blocks_per_seq + (blocks_per_seq - 1 - n)


def _states_c(lg, v, msel, batch, seq):
    T = v.shape[1]
    nb = seq // STATE_BLOCK
    cpb = STATE_BLOCK // CHUNK
    blk = _reversed_block_index(nb)
    return pl.pallas_call(
        _state_kernel_c,
        grid=(batch, nb),
        in_specs=[
            pl.BlockSpec((C_HEADS, STATE_BLOCK, LANES), lambda b, n: (1, blk(b, n), 0)),
            pl.BlockSpec((C_HEADS, STATE_BLOCK, LANES), lambda b, n: (0, blk(b, n), 0)),
            pl.BlockSpec(msel.shape, lambda b, n: (0, 0)),
        ],
        out_specs=pl.BlockSpec((cpb, C_HEADS, C_HEAD_DIM, C_HEAD_DIM), lambda b, n: (blk(b, n), 0, 0, 0)),
        out_shape=jax.ShapeDtypeStruct((T // CHUNK, C_HEADS, C_HEAD_DIM, C_HEAD_DIM), BF16),
        scratch_shapes=[pltpu.VMEM((C_HEADS, C_HEAD_DIM, C_HEAD_DIM), F32),
                        pltpu.VMEM((C_HEADS, STATE_BLOCK, LANES), BF16),
                        pltpu.VMEM((C_HEADS, cpb * TOTAL_ROWS, LANES), F32)],
        compiler_params=_params(("parallel", "arbitrary")),
        name="hgrn_states_bwd",
    )(lg, v, msel)


def _states_b(lr, bqk, bv, up_pad, bias_pad, msel, batch, seq):
    T = lr.shape[0]
    nb = seq // STATE_BLOCK
    cpb = STATE_BLOCK // CHUNK
    blk = _reversed_block_index(nb)
    full = lambda a: pl.BlockSpec(a.shape, lambda b, n: (0,) * a.ndim)
    return pl.pallas_call(
        _state_kernel_b,
        grid=(batch, nb),
        in_specs=[
            pl.BlockSpec((STATE_BLOCK, LANES), lambda b, n: (blk(b, n), 0)),
            pl.BlockSpec((B_HEADS, STATE_BLOCK, LANES), lambda b, n: (1, blk(b, n), 0)),
            pl.BlockSpec((2 * B_HEADS, STATE_BLOCK, LANES), lambda b, n: (0, blk(b, n), 0)),
            full(up_pad), full(bias_pad), full(msel),
        ],
        out_specs=pl.BlockSpec((cpb, B_HEADS, B_VAL_DIM, B_KEY_DIM), lambda b, n: (blk(b, n), 0, 0, 0)),
        out_shape=jax.ShapeDtypeStruct((T // CHUNK, B_HEADS, B_VAL_DIM, B_KEY_DIM), BF16),
        scratch_shapes=[pltpu.VMEM((B_HEADS, B_VAL_DIM, B_KEY_DIM), F32),
                        pltpu.VMEM((2 * B_HEADS, STATE_BLOCK, LANES), F32),
                        pltpu.VMEM((B_HEADS, STATE_BLOCK, LANES), BF16),
                        pltpu.VMEM((B_HEADS, cpb * TOTAL_ROWS, LANES), F32)],
        compiler_params=_params(("parallel", "arbitrary")),
        name="gla_states_bwd",
    )(lr, bqk, bv, up_pad, bias_pad, msel)


def _project_and_norm(y_scr, wout_ref, x_ref, lng_ref, lnb_ref, out_ref):
    y = jnp.concatenate([y_scr[g] for g in range(MIX_GROUPS)], axis=1)
    z = DEEPNORM_ALPHA * x_ref[...] + _dot(y, wout_ref[...])
    out_ref[...] = _layer_norm(z, lng_ref[...], lnb_ref[...])


def _rms_gain(o, gain):
    return o * lax.rsqrt(jnp.mean(o * o, axis=-1, keepdims=True) + NORM_EPS) * gain


def _scan_heads(n_heads, nv, lg_of, q_rows_of, k_rows_of, v_ref, st_ref, sb_ref, mpre, msuf, scan_scr, finish):
    masks = _intra_masks()

    cum_scr, op_scr, a_scr, ds_scr, stf_scr = scan_scr

    @pl.when(pl.program_id(1) == 0)
    def _():
        st_ref[...] = jnp.zeros_like(st_ref)

    per_step = cum_scr.shape[0]

    def step(i, carry):
        heads = [i * per_step + j for j in range(per_step)]
        for j, h in enumerate(heads):
            cum_scr[j, 0] = _dot_sel(mpre, lg_of(h, 0))
            cum_scr[j, 1] = _dot_sel(msuf, lg_of(h, 1))
        for j, h in enumerate(heads):
            for d in range(2):
                _decayed_operands(cum_scr.at[j, d], q_rows_of(h), k_rows_of(h, d), op_scr.at[j, d], d == 1)
        units = [(j, h, ci) for j, h in enumerate(heads) for ci in range(ROW_BLOCK // CHUNK)]
        chunk_rows = lambda ci: slice(ci * CHUNK, (ci + 1) * CHUNK)
        for j, h, ci in units:
            a_scr[j, ci] = _intra_matrix(ci, op_scr.at[j, 0], op_scr.at[j, 1], masks)
            ds_scr[j, ci] = _dot_tn(_head_value(v_ref, h, chunk_rows(ci), nv),
                                    op_scr[j, 0, N_OPERANDS - 1, chunk_rows(ci), :])
        for j, h in enumerate(heads):
            st = st_ref[h]
            for ci in range(ROW_BLOCK // CHUNK):
                stf_scr[j, ci] = st.astype(BF16)
                total = cum_scr[j, 0, (ci + 1) * CHUNK - 1:(ci + 1) * CHUNK, :]
                st = st * jnp.exp2(total) + ds_scr[j, ci]
            st_ref[h] = st
        for j, h, ci in units:
            o = _chunk_output(ci, a_scr[j, ci], op_scr.at[j, 0], op_scr.at[j, 1],
                              _head_value(v_ref, h, chunk_rows(ci), nv), stf_scr[j, ci], sb_ref[ci, h])
            finish(h, chunk_rows(ci), o)
        return carry

    lax.fori_loop(0, n_heads // per_step, step, 0)


def _out_kernel_c(q_ref, i_ref, gate_ref, lgf_ref, lgb_ref, sb_ref, mpre_ref, msuf_ref, gain_ref,
                  wout_ref, x_ref, lng_ref, lnb_ref, out_ref, y_scr, st_scr, *scan_scr):
    lg_refs = (lgf_ref, lgb_ref)

    def finish(h, rows, o):
        y = _rms_gain(o, gain_ref[h]) * _silu(gate_ref[h, rows, :].astype(F32))
        y_scr[h, rows, :] = y.astype(BF16)

    _scan_heads(C_HEADS, 1,
                lambda h, d: lg_refs[d][h],
                lambda h: lambda rows: q_ref[h, rows, :].astype(F32),
                lambda h, d: lambda rows: 1.0 - jnp.exp2(lg_refs[d][h, rows, :]),
                i_ref, st_scr, sb_ref, mpre_ref[...], msuf_ref[...], scan_scr, finish)
    _project_and_norm(y_scr, wout_ref, x_ref, lng_ref, lnb_ref, out_ref)


def _out_kernel_ab(o1_ref, o2_ref, o3_ref, l1_ref, l2_ref, l3_ref, qk_ref, v_ref, gate_ref, lr_ref,
                   sb_ref, mf_ref, mb_ref, up_ref, gbias_ref, gain_ref,
                   wout_ref, x_ref, lng_ref, lnb_ref, out_ref, y_scr, lg_scr, st_scr, *scan_scr):
    for g in range(A_GROUPS):
        l1, l2, l3 = l1_ref[g], l2_ref[g], l3_ref[g]
        m = jnp.maximum(jnp.maximum(l1, l2), l3)
        e1, e2, e3 = jnp.exp2(l1 - m), jnp.exp2(l2 - m), jnp.exp2(l3 - m)
        num = e1 * o1_ref[g].astype(F32) + e2 * o2_ref[g].astype(F32) + e3 * o3_ref[g].astype(F32)
        oa = num / (e1 + e2 + e3)
        y_scr[g] = (oa * _silu(gate_ref[g].astype(F32))).astype(BF16)

    _gla_log_gates(lr_ref[...], up_ref, gbias_ref, lg_scr, directions=(0, 1))
    nv = B_VAL_DIM // LANES

    def finish(h, rows, o):
        gain = jnp.concatenate([gain_ref[nv * h + k] for k in range(nv)], axis=1)
        o = _rms_gain(o, gain)
        for kk in range(nv):
            grp = A_GROUPS + nv * h + kk
            y = o[:, kk * LANES:(kk + 1) * LANES] * _silu(gate_ref[grp, rows, :].astype(F32))
            y_scr[grp, rows, :] = y.astype(BF16)

    _scan_heads(B_HEADS, nv,
                lambda h, d: lg_scr[d * B_HEADS + h],
                lambda h: lambda rows: qk_ref[h, rows, :].astype(F32),
                lambda h, d: lambda rows: qk_ref[B_HEADS + h, rows, :].astype(F32),
                v_ref, st_scr, sb_ref, mf_ref[...], mb_ref[...], scan_scr, finish)
    _project_and_norm(y_scr, wout_ref, x_ref, lng_ref, lnb_ref, out_ref)


def _scan_scratch(tb, n_heads, v_dim, k_dim):
    cpb = tb // CHUNK
    per_step = min(HEADS_PER_STEP, n_heads)
    assert n_heads % per_step == 0
    return [pltpu.VMEM((per_step, 2, tb, LANES), F32),
            pltpu.VMEM((per_step, 2, N_OPERANDS, tb, LANES), BF16),
            pltpu.VMEM((per_step, cpb, CHUNK, CHUNK), BF16),
            pltpu.VMEM((per_step, cpb, v_dim, k_dim), F32),
            pltpu.VMEM((per_step, cpb, v_dim, k_dim), BF16)]


def _row_specs(tb, nb):
    grp = lambda n, part=0: pl.BlockSpec((n, tb, LANES), lambda b, i: (part, b * nb + i, 0))
    full = lambda a: pl.BlockSpec(a.shape, lambda b, i: (0,) * a.ndim)
    rows = lambda width: pl.BlockSpec((tb, width), lambda b, i: (b * nb + i, 0))
    return grp, full, rows


def _output_c(cq, ci_, gate, lg, sb, mf, mb, gain, wout, x2d, lng, lnb, batch, seq):
    T = x2d.shape[0]
    tb = ROW_BLOCK
    nb = seq // tb
    cpb = tb // CHUNK
    grp, full, rows = _row_specs(tb, nb)
    st = pl.BlockSpec((cpb, C_HEADS, C_HEAD_DIM, C_HEAD_DIM), lambda b, i: (b * nb + i, 0, 0, 0))
    return pl.pallas_call(
        _out_kernel_c,
        grid=(batch, nb),
        in_specs=[grp(C_HEADS), grp(C_HEADS), grp(C_HEADS), grp(C_HEADS, 0), grp(C_HEADS, 1), st,
                  full(mf), full(mb), full(gain), full(wout), rows(D_MODEL), full(lng), full(lnb)],
        out_specs=rows(D_MODEL),
        out_shape=jax.ShapeDtypeStruct((T, D_MODEL), F32),
        scratch_shapes=[pltpu.VMEM((MIX_GROUPS, tb, LANES), BF16),
                        pltpu.VMEM((C_HEADS, C_HEAD_DIM, C_HEAD_DIM), F32)]
        + _scan_scratch(tb, C_HEADS, C_HEAD_DIM, C_HEAD_DIM),
        compiler_params=_params(("parallel", "arbitrary")),
        name="hgrn_output",
    )(cq, ci_, gate, lg, lg, sb, mf, mb, gain, wout, x2d, lng, lnb)


def _output_ab(attn, bqk, bv, gate, lr, sb, mf, mb, up_pad, gbias_pad, gain, wout, x2d, lng, lnb, batch, seq):
    T = x2d.shape[0]
    tb = ROW_BLOCK
    nb = seq // tb
    cpb = tb // CHUNK
    grp, full, rows = _row_specs(tb, nb)
    st = pl.BlockSpec((cpb, B_HEADS, B_VAL_DIM, B_KEY_DIM), lambda b, i: (b * nb + i, 0, 0, 0))
    (o1, l1), (o2, l2), (o3, l3) = attn
    return pl.pallas_call(
        _out_kernel_ab,
        grid=(batch, nb),
        in_specs=[grp(A_GROUPS)] * 6 + [grp(2 * B_HEADS), grp(8), grp(MIX_GROUPS), rows(LANES), st,
                                        full(mf), full(mb), full(up_pad), full(gbias_pad), full(gain),
                                        full(wout), rows(D_MODEL), full(lng), full(lnb)],
        out_specs=rows(D_MODEL),
        out_shape=jax.ShapeDtypeStruct((T, D_MODEL), F32),
        scratch_shapes=[pltpu.VMEM((MIX_GROUPS, tb, LANES), BF16),
                        pltpu.VMEM((2 * B_HEADS, tb, LANES), F32),
                        pltpu.VMEM((B_HEADS, B_VAL_DIM, B_KEY_DIM), F32)]
        + _scan_scratch(tb, B_HEADS, B_VAL_DIM, B_KEY_DIM),
        compiler_params=_params(("parallel", "arbitrary")),
        name="ab_output",
    )(o1, o2, o3, l1, l2, l3, bqk, bv, gate, lr, sb, mf, mb, up_pad, gbias_pad, gain, wout, x2d, lng, lnb)


def _layer_ab(x2d, batch, seq, w_in, gate_up, gate_bias, norm_gain, w_out, bias_tbls, lng, lnb, consts):
    mf, mb, msel = consts
    sizes = (3 * A_WIDTH + 2 * B_K_WIDTH + B_V_WIDTH, 2 * B_GATE_RANK, MIX_WIDTH)
    o0, o1 = sizes[0], sizes[0] + sizes[1]
    w_main = jnp.concatenate([w_in[:, :o0], w_in[:, o1:]], axis=1).astype(BF16)
    w_lr = jnp.pad(w_in[:, o0:o1], ((0, 0), (0, LANES - sizes[1]))).astype(BF16)
    up = gate_up.astype(F32)
    up_pad = jnp.zeros((LANES, 2 * B_K_WIDTH), F32)
    up_pad = up_pad.at[0:B_GATE_RANK, 0:B_K_WIDTH].set(up[0])
    up_pad = up_pad.at[B_GATE_RANK:2 * B_GATE_RANK, B_K_WIDTH:].set(up[1])
    gbias_pad = gate_bias.astype(F32).reshape(1, 2 * B_K_WIDTH)

    *a_splits, bqk, bv, gate, lr = _inproj_ab(x2d, w_main, w_lr, batch, seq)
    attn = [_attention_pattern(a, tbl, d) for a, tbl, d in zip(a_splits, bias_tbls, DILATIONS)]
    sb = _states_b(lr, bqk, bv, up_pad, gbias_pad, msel, batch, seq)
    gain = norm_gain.astype(F32).reshape(B_V_WIDTH // LANES, 1, LANES)
    wout = w_out.astype(BF16)
    return _output_ab(attn, bqk, bv, gate, lr, sb, mf, mb, up_pad, gbias_pad, gain, wout, x2d, lng, lnb,
                      batch, seq)


def _layer_c(x2d, batch, seq, w_in, lower_bounds, layer_idx, norm_gain, w_out, lng, lnb, consts):
    mf, mb, msel = consts
    cq, ci_, gate, lg = _inproj_c(x2d, w_in.astype(BF16), lower_bounds.astype(F32), layer_idx)
    sb = _states_c(lg, ci_, msel, batch, seq)
    gain = norm_gain.astype(F32).reshape(C_HEADS, 1, LANES)
    wout = w_out.astype(BF16)
    return _output_c(cq, ci_, gate, lg, sb, mf, mb, gain, wout, x2d, lng, lnb, batch, seq)


def kernel(x, w_in_ab, gla_gate_up, gla_gate_bias, gla_norm, w_out_ab, w_in_c, hgrn_lower_bounds, hgrn_norm,
           w_out_c, rel_bias, ln_gain, ln_bias):
    batch, seq, _ = x.shape
    assert seq % ATT_TOKENS == 0 and seq % ROW_BLOCK == 0 and seq % STATE_BLOCK == 0 and seq % PROJ_ROWS == 0
    consts = _scan_constants()
    bias_tbls = [_bias_tables(rel_bias, d) for d in DILATIONS]
    x2d = x.astype(F32).reshape(batch * seq, D_MODEL)
    for layer in range(DEPTH):
        lng = ln_gain[layer].astype(F32).reshape(1, D_MODEL)
        lnb = ln_bias[layer].astype(F32).reshape(1, D_MODEL)
        if layer % 2 == 0:
            e = layer // 2
            x2d = _layer_ab(x2d, batch, seq, w_in_ab[e], gla_gate_up[e], gla_gate_bias[e], gla_norm[e],
                            w_out_ab[e], bias_tbls, lng, lnb, consts)
        else:
            o = layer // 2
            x2d = _layer_c(x2d, batch, seq, w_in_c[o], hgrn_lower_bounds, layer, hgrn_norm[o], w_out_c[o],
                           lng, lnb, consts)
    return x2d.reshape(batch, seq, D_MODEL).astype(x.dtype)
```
